```python
import math
import jax, jax.numpy as jnp
from jax import lax
import numpy as np

D_MODEL = 1024
BATCH = 8
SEQ = 2048
DEPTH = 4

N_META = 16
RMS_EPS = 1e-6
L2_EPS = 1e-6
CONV_K = 4
SB_HEADS = 8
SB_HEAD_DIM = 128
SB_WIDTH = SB_HEADS * SB_HEAD_DIM
SB_BLOCK = 128
GDN_HEADS = 8
GDN_DK = 128
GDN_DV = 128
GDN_QK_WIDTH = GDN_HEADS * GDN_DK
GDN_V_WIDTH = GDN_HEADS * GDN_DV
GDN_CHUNK = 64
SSM_EXPAND = 2
SSM_INNER = SSM_EXPAND * D_MODEL
SSM_HEAD_DIM = 64
SSM_HEADS = SSM_INNER // SSM_HEAD_DIM
SSM_GROUPS = 2
SSM_STATE = 128
SSM_CHUNK = 64
SSM_CONV_CH = SSM_INNER + 2 * SSM_GROUPS * SSM_STATE
N_BRANCH = 3
IN_SPLITS = (SB_WIDTH, SB_WIDTH, SB_WIDTH, SB_WIDTH,
             GDN_QK_WIDTH, GDN_QK_WIDTH, GDN_V_WIDTH, GDN_V_WIDTH, GDN_HEADS, GDN_HEADS,
             SSM_INNER, SSM_CONV_CH, SSM_HEADS,
             N_BRANCH * D_MODEL)
D_IN = 4 * SB_WIDTH + 2 * GDN_QK_WIDTH + 2 * GDN_V_WIDTH + 2 * GDN_HEADS + SSM_INNER + SSM_CONV_CH + SSM_HEADS + N_BRANCH * D_MODEL

kernel_name = 'hybrid_stickbreak_gdn_ssd_block'


def _split(t, sizes):
    out, start = [], 0
    for s in sizes:
        out.append(t[..., start:start + s])
        start += s
    return out


def _rmsnorm(x, g):
    xf = x.astype(jnp.float32)
    y = xf * lax.rsqrt(jnp.mean(xf * xf, axis=-1, keepdims=True) + RMS_EPS)
    return (y * g.astype(jnp.float32)).astype(x.dtype)


def _l2norm(t):
    t = t.astype(jnp.float32)
    return t * lax.rsqrt(jnp.sum(t * t, axis=-1, keepdims=True) + L2_EPS)


def _front_pad(t, n):
    return jnp.pad(t, [(0, 0), (n, 0)] + [(0, 0)] * (t.ndim - 2))


def _causal_dwconv(x, w):
    k, c = w.shape
    return lax.conv_general_dilated(x, w.astype(x.dtype).reshape(k, 1, c), window_strides=(1,),
                                    padding=[(k - 1, 0)], dimension_numbers=('NWC', 'WIO', 'NWC'),
                                    feature_group_count=c)


def _stick_breaking_attention(q, k, v):
    bsz, seq, nh, dh = q.shape
    pad = SB_BLOCK - N_META
    lp = seq + pad
    nb = lp // SB_BLOCK
    scale = dh ** -0.5
    qf, kf, vf = [jnp.swapaxes(_front_pad(t.astype(jnp.float32), pad), 1, 2) for t in (q, k, v)]
    q_blocks = jnp.moveaxis(qf.reshape(bsz, nh, nb, SB_BLOCK, dh), 2, 0)
    key_pos = jnp.arange(lp)

    def one_block(args):
        qb, bi = args
        q_pos = bi * SB_BLOCK + jnp.arange(SB_BLOCK)
        mask = (key_pos[None, :] < q_pos[:, None]) & (key_pos[None, :] >= pad)
        z = jnp.einsum('bhqd,bhkd->bhqk', qb, kf) * scale
        log_keep = jnp.where(mask, jax.nn.log_sigmoid(-z), 0.0)
        later = lax.cumsum(log_keep, axis=3, reverse=True) - log_keep
        w = jnp.where(mask, jnp.exp(jax.nn.log_sigmoid(z) + later), 0.0)
        return jnp.einsum('bhqk,bhkd->bhqd', w, vf)

    out = lax.map(one_block, (q_blocks, jnp.arange(nb)))
    out = jnp.transpose(out, (1, 0, 3, 2, 4)).reshape(bsz, lp, nh, dh)
    return out[:, pad:]


def _gated_delta_rule(q, k, v, g, beta):
    pad = GDN_CHUNK - N_META
    q, k, v, g, beta = [_front_pad(t, pad) for t in (q, k, v, g, beta)]
    bsz, lp, nh, dk = q.shape
    dv = v.shape[-1]
    cl = GDN_CHUNK
    nc = lp // cl

    def chunked(t):
        return t.reshape((bsz, nc, cl) + t.shape[2:])

    q = chunked(q * dk ** -0.5)
    k, v, g, beta = chunked(k), chunked(v), chunked(g), chunked(beta)
    gc = jnp.cumsum(g, axis=2)
    gt = jnp.moveaxis(gc, 2, 3)
    seg = gt[..., :, None] - gt[..., None, :]
    idx = jnp.arange(cl)
    strict = idx[:, None] > idx[None, :]
    incl = idx[:, None] >= idx[None, :]
    dec_strict = jnp.exp(jnp.where(strict, seg, -jnp.inf))
    dec_incl = jnp.exp(jnp.where(incl, seg, -jnp.inf))
    kb = k * beta[..., None]
    m = jnp.einsum('bclhd,bcshd->bchls', kb, k) * dec_strict
    eye = jnp.eye(cl, dtype=m.dtype)
    t_inv = lax.linalg.triangular_solve(m + eye, jnp.broadcast_to(eye, m.shape), left_side=True,
                                        lower=True, unit_diagonal=True)
    u = jnp.einsum('bchls,bcshd->bclhd', t_inv, v * beta[..., None])
    w = jnp.einsum('bchls,bcshd->bclhd', t_inv, kb * jnp.exp(gc)[..., None])
    a_qk = jnp.einsum('bclhd,bcshd->bchls', q, k) * dec_incl
    g_last = gc[:, :, -1]
    q_dec = q * jnp.exp(gc)[..., None]
    k_end = k * jnp.exp(g_last[:, :, None] - gc)[..., None]

    def step(state, inp):
        qd, ke, uc, wc, aqk, gl = inp
        v_new = uc - jnp.einsum('blhk,bhkv->blhv', wc, state)
        o = jnp.einsum('blhk,bhkv->blhv', qd, state) + jnp.einsum('bhls,bshv->blhv', aqk, v_new)
        state = state * jnp.exp(gl)[..., None, None] + jnp.einsum('blhk,blhv->bhkv', ke, v_new)
        return state, o

    s0 = jnp.zeros((bsz, nh, dk, dv), q.dtype)
    xs = tuple(jnp.moveaxis(t, 1, 0) for t in (q_dec, k_end, u, w, a_qk, g_last))
    _, o = lax.scan(step, s0, xs)
    o = jnp.moveaxis(o, 0, 1).reshape(bsz, lp, nh, dv)
    return o[:, pad:]


def _ssd_scan(x, dt, a, b_in, c_in):
    pad = SSM_CHUNK - N_META
    x, dt, b_in, c_in = [_front_pad(t, pad) for t in (x, dt, b_in, c_in)]
    bsz, lp, nh, p = x.shape
    ng, n = b_in.shape[2], b_in.shape[3]
    hg = nh // ng
    cl = SSM_CHUNK
    nc = lp // cl
    xs = (x * dt[..., None]).reshape(bsz, nc, cl, ng, hg, p)
    la = (dt * a).reshape(bsz, nc, cl, ng, hg)
    bc = b_in.reshape(bsz, nc, cl, ng, n)
    cc = c_in.reshape(bsz, nc, cl, ng, n)
    cs = jnp.cumsum(la, axis=2)
    causal = jnp.tril(jnp.ones((cl, cl), dtype=bool))
    seg = cs[:, :, :, None] - cs[:, :, None, :]
    decay = jnp.exp(jnp.where(causal[:, :, None, None], seg, -jnp.inf))
    scores = jnp.einsum('bclgn,bcsgn->bclsg', cc, bc)[..., None] * decay
    y_diag = jnp.einsum('bclsgh,bcsghp->bclghp', scores, xs)
    to_end = jnp.exp(cs[:, :, -1:] - cs)
    states = jnp.einsum('bclgn,bclghp->bcghpn', bc, xs * to_end[..., None])
    chunk_decay = jnp.exp(cs[:, :, -1])

    def step(hst, inp):
        st, cd = inp
        return cd[..., None, None] * hst + st, hst

    h0 = jnp.zeros((bsz, ng, hg, p, n), xs.dtype)
    _, h_prev = lax.scan(step, h0, (jnp.moveaxis(states, 1, 0), jnp.moveaxis(chunk_decay, 1, 0)))
    h_prev = jnp.moveaxis(h_prev, 0, 1)
    y_off = jnp.einsum('bclgn,bcghpn->bclghp', cc, h_prev) * jnp.exp(cs)[..., None]
    y = (y_diag + y_off).reshape(bsz, lp, nh, p)
    return y[:, pad:]


def _hybrid_mixer(u, w_in, gdn_conv_w, gdn_a_log, gdn_dt_bias, gdn_norm_g, ssm_conv_w, ssm_conv_b,
                  ssm_a_log, ssm_dt_bias, ssm_d, ssm_norm_g, w_branch_a, w_branch_b, w_branch_c, w_out):
    f32 = jnp.float32
    dtype = u.dtype
    bsz, seq, _ = u.shape
    proj = u @ w_in
    (sb_q, sb_k, sb_v, sb_z, gd_q, gd_k, gd_v, gd_z, gd_b, gd_a,
     ss_z, ss_xbc, ss_dt, gates) = _split(proj, IN_SPLITS)

    def heads(t, nh):
        return t.reshape(bsz, seq, nh, -1)

    o_a = _stick_breaking_attention(heads(sb_q, SB_HEADS), heads(sb_k, SB_HEADS), heads(sb_v, SB_HEADS))
    o_a = o_a.reshape(bsz, seq, SB_WIDTH).astype(dtype) * jax.nn.silu(sb_z)

    qkv = jax.nn.silu(_causal_dwconv(jnp.concatenate([gd_q, gd_k, gd_v], axis=-1), gdn_conv_w))
    cq, ck, cv = _split(qkv, (GDN_QK_WIDTH, GDN_QK_WIDTH, GDN_V_WIDTH))
    beta = jax.nn.sigmoid(gd_b.astype(f32))
    g = -jnp.exp(gdn_a_log.astype(f32)) * jax.nn.softplus(gd_a.astype(f32) + gdn_dt_bias.astype(f32))
    o_b = _gated_delta_rule(_l2norm(heads(cq, GDN_HEADS)), _l2norm(heads(ck, GDN_HEADS)),
                            heads(cv, GDN_HEADS).astype(f32), g, beta)
    o_b = _rmsnorm(o_b, gdn_norm_g).reshape(bsz, seq, GDN_V_WIDTH).astype(dtype) * jax.nn.silu(gd_z)

    xbc = jax.nn.silu(_causal_dwconv(ss_xbc, ssm_conv_w) + ssm_conv_b)
    sx, sb, sc = _split(xbc, (SSM_INNER, SSM_GROUPS * SSM_STATE, SSM_GROUPS * SSM_STATE))
    dt = jax.nn.softplus(ss_dt.astype(f32) + ssm_dt_bias.astype(f32))
    a = -jnp.exp(ssm_a_log.astype(f32))
    xh = heads(sx, SSM_HEADS).astype(f32)
    y = _ssd_scan(xh, dt, a, heads(sb, SSM_GROUPS).astype(f32), heads(sc, SSM_GROUPS).astype(f32))
    y = y + ssm_d.astype(f32)[:, None] * xh
    y = y.reshape(bsz, seq, SSM_INNER) * jax.nn.silu(ss_z.astype(f32))
    o_c = _rmsnorm(y.reshape(bsz, seq, SSM_GROUPS, -1), ssm_norm_g.reshape(SSM_GROUPS, -1))
    o_c = o_c.reshape(bsz, seq, SSM_INNER).astype(dtype)

    g_a, g_b, g_c = _split(jax.nn.sigmoid(gates), (D_MODEL, D_MODEL, D_MODEL))
    merged = g_a * (o_a @ w_branch_a) + g_b * (o_b @ w_branch_b) + g_c * (o_c @ w_branch_c)
    return merged @ w_out


def setup_inputs(seed: int = 0) -> dict:
    key = jax.random.key(seed)
    ks = jax.random.split(key, 20)
    f32 = jnp.float32

    def nrm(k, shape, scale):
        return jax.random.normal(k, shape, f32) * scale

    def gain(k, shape):
        return 1.0 + 0.02 * jax.random.normal(k, shape, f32)

    def dt_bias(k, shape):
        dt = jnp.exp(jax.random.uniform(k, shape, f32, math.log(1e-3), math.log(1e-1)))
        return dt + jnp.log(-jnp.expm1(-dt))

    def a_log(k, shape):
        return jnp.log(jax.random.uniform(k, shape, f32, 1.0, 16.0))

    return {
        'x': jax.random.normal(ks[0], (BATCH, SEQ, D_MODEL), f32),
        'meta_tokens': nrm(ks[1], (N_META, D_MODEL), 1.0),
        'norm_g': gain(ks[2], (DEPTH, D_MODEL)),
        'w_in': nrm(ks[3], (DEPTH, D_MODEL, D_IN), D_MODEL ** -0.5),
        'gdn_conv_w': nrm(ks[4], (DEPTH, CONV_K, 2 * GDN_QK_WIDTH + GDN_V_WIDTH), CONV_K ** -0.5),
        'gdn_a_log': a_log(ks[5], (DEPTH, GDN_HEADS)),
        'gdn_dt_bias': dt_bias(ks[6], (DEPTH, GDN_HEADS)),
        'gdn_norm_g': gain(ks[7], (DEPTH, GDN_DV)),
        'ssm_conv_w': nrm(ks[8], (DEPTH, CONV_K, SSM_CONV_CH), CONV_K ** -0.5),
        'ssm_conv_b': nrm(ks[9], (DEPTH, SSM_CONV_CH), 0.01),
        'ssm_a_log': a_log(ks[10], (DEPTH, SSM_HEADS)),
        'ssm_dt_bias': dt_bias(ks[11], (DEPTH, SSM_HEADS)),
        'ssm_d': 1.0 + 0.1 * jax.random.normal(ks[12], (DEPTH, SSM_HEADS), f32),
        'ssm_norm_g': gain(ks[13], (DEPTH, SSM_INNER)),
        'w_branch_a': nrm(ks[14], (DEPTH, SB_WIDTH, D_MODEL), SB_WIDTH ** -0.5),
        'w_branch_b': nrm(ks[15], (DEPTH, GDN_V_WIDTH, D_MODEL), GDN_V_WIDTH ** -0.5),
        'w_branch_c': nrm(ks[16], (DEPTH, SSM_INNER, D_MODEL), SSM_INNER ** -0.5),
        'w_out': nrm(ks[17], (DEPTH, D_MODEL, D_MODEL), D_MODEL ** -0.5),
        'final_norm_g': gain(ks[18], (D_MODEL,)),
    }


def reference(x, meta_tokens, norm_g, w_in, gdn_conv_w, gdn_a_log, gdn_dt_bias, gdn_norm_g, ssm_conv_w,
              ssm_conv_b, ssm_a_log, ssm_dt_bias, ssm_d, ssm_norm_g, w_branch_a, w_branch_b, w_branch_c,
              w_out, final_norm_g):
    bsz = x.shape[0]
    meta = jnp.broadcast_to(meta_tokens.astype(x.dtype)[None], (bsz, N_META, D_MODEL))
    h = jnp.concatenate([meta, x], axis=1)
    for layer in range(DEPTH):
        h = h + _hybrid_mixer(_rmsnorm(h, norm_g[layer]), w_in[layer], gdn_conv_w[layer], gdn_a_log[layer],
                              gdn_dt_bias[layer], gdn_norm_g[layer], ssm_conv_w[layer], ssm_conv_b[layer],
                              ssm_a_log[layer], ssm_dt_bias[layer], ssm_d[layer], ssm_norm_g[layer],
                              w_branch_a[layer], w_branch_b[layer], w_branch_c[layer], w_out[layer])
    return _rmsnorm(h, final_norm_g)[:, N_META:]
```

```python
import functools

import jax
import jax.numpy as jnp
from jax import lax
from jax.experimental import pallas as pl
from jax.experimental.pallas import tpu as pltpu

F32 = jnp.float32
BF16 = jnp.bfloat16

N_META = 16
RMS_EPS = 1e-6
L2_EPS = 1e-6
CONV_K = 4
HEAD_DIM = 128
SSM_GROUPS = 2
SSM_STATE = 128
SSM_P = 64
CHUNK = 64
LANE = 128
ATT_BLOCK = 256
VMEM_LIMIT = 52 * 1024 * 1024


def _cparams(*sem):
    return pltpu.CompilerParams(dimension_semantics=sem, vmem_limit_bytes=VMEM_LIMIT)


def _sigmoid(x):
    return 1.0 / (1.0 + jnp.exp(-x))


def _silu(x):
    return x * _sigmoid(x)


def _softplus(x):
    return jnp.maximum(x, 0.0) + jnp.log1p(jnp.exp(-jnp.abs(x)))


def _split3(x):
    hi = x.astype(BF16)
    r = x - hi.astype(F32)
    mid = r.astype(BF16)
    lo = (r - mid.astype(F32)).astype(BF16)
    return hi, mid, lo


def _dot(a, b):
    return jnp.dot(a, b, preferred_element_type=F32)


def _dot_nt(a, b):
    return lax.dot_general(a, b, (((1,), (1,)), ((), ())), preferred_element_type=F32)


def _dot_tn(a, b):
    return lax.dot_general(a, b, (((0,), (0,)), ((), ())), preferred_element_type=F32)


def _dot_hp(a, b):
    ah = a.astype(BF16)
    al = (a - ah.astype(F32)).astype(BF16)
    bh = b.astype(BF16)
    bl = (b - bh.astype(F32)).astype(BF16)
    return _dot(ah, bh) + (_dot(al, bh) + _dot(ah, bl))


def _pick(n, cands):
    for c in cands:
        if n % c == 0:
            return c
    raise ValueError(f"no tile for {n} in {cands}")


def _rmsnorm_kernel(h_ref, g_ref, o_ref):
    x = h_ref[...]
    y = x * lax.rsqrt(jnp.mean(x * x, axis=-1, keepdims=True) + RMS_EPS) * g_ref[...]
    o_ref[...] = y.astype(o_ref.dtype)


def _rmsnorm(h, g, out_dtype):
    tp, d = h.shape
    tm = _pick(tp, (512, 384, 256, 128))
    return pl.pallas_call(
        _rmsnorm_kernel,
        grid=(tp // tm,),
        in_specs=[pl.BlockSpec((tm, d), lambda i: (i, 0)), pl.BlockSpec((1, d), lambda i: (0, 0))],
        out_specs=pl.BlockSpec((tm, d), lambda i: (i, 0)),
        out_shape=jax.ShapeDtypeStruct((tp, d), out_dtype),
        compiler_params=_cparams("parallel"),
        name="rmsnorm",
    )(h, g.reshape(1, d))


def _inproj_plain_kernel(x_ref, w_ref, o_ref):
    o_ref[...] = _dot(x_ref[...], w_ref[...]).astype(o_ref.dtype)


def _inproj_conv_kernel(x_ref, w_ref, cw_ref, cb_ref, o_ref, *, pad, norm_blocks, q_scale):
    acc = _dot(x_ref[...], w_ref[...])
    cw = cw_ref[...]
    y = acc * cw[CONV_K - 1:CONV_K, :]
    for k in range(CONV_K - 1):
        y = y + pltpu.roll(acc, CONV_K - 1 - k, axis=0) * cw[k:k + 1, :]
    y = _silu(y + cb_ref[...])
    row = lax.broadcasted_iota(jnp.int32, y.shape, 0)
    y = jnp.where(row >= pad, y, 0.0)
    if norm_blocks:
        j = pl.program_id(1)
        for s in range(y.shape[1] // HEAD_DIM):
            ys = y[:, s * HEAD_DIM:(s + 1) * HEAD_DIM]
            rs = lax.rsqrt(jnp.sum(ys * ys, axis=1, keepdims=True) + L2_EPS)
            fac = jnp.where(j < norm_blocks, rs * q_scale, jnp.where(j < 2 * norm_blocks, rs, 1.0))
            o_ref[:, s * HEAD_DIM:(s + 1) * HEAD_DIM] = (ys * fac).astype(o_ref.dtype)
    else:
        o_ref[...] = y.astype(o_ref.dtype)


def _inproj_small_kernel(x_ref, w_ref, p_ref, o_ref, *, pad):
    v = _dot(x_ref[...], w_ref[...])
    kind = p_ref[0:1, :]
    alog = p_ref[1:2, :]
    bias = p_ref[2:3, :]
    sp = _softplus(v + bias)
    out = jnp.where(kind == 0.0, _sigmoid(v),
                    jnp.where(kind == 1.0, -jnp.exp(alog) * sp, jnp.where(kind == 2.0, sp, 0.0)))
    row = lax.broadcasted_iota(jnp.int32, out.shape, 0)
    o_ref[...] = jnp.where(row >= pad, out, 0.0)


def _inproj(xn, w, bsz, lp, *, tn, kernel, extra=(), extra_specs=(), out_dtype=F32, name):
    d = xn.shape[1]
    n = w.shape[1]
    return pl.pallas_call(
        kernel,
        grid=(bsz, n // tn),
        in_specs=[pl.BlockSpec((lp, d), lambda b, j: (b, 0)),
                  pl.BlockSpec((d, tn), lambda b, j: (0, j)), *extra_specs],
        out_specs=pl.BlockSpec((lp, tn), lambda b, j: (b, j)),
        out_shape=jax.ShapeDtypeStruct((bsz * lp, n), out_dtype),
        compiler_params=_cparams("parallel", "arbitrary"),
        name=name,
    )(xn, w, *extra)


def _attn_kernel(q_ref, k_ref, v_ref, z_ref, uu_ref, o_ref, *, blk, pad, scale):
    i = pl.program_id(2)
    q = (q_ref[...] * scale).astype(BF16)
    qpos = i * blk + lax.broadcasted_iota(jnp.int32, (blk, blk), 0)
    lane = lax.broadcasted_iota(jnp.int32, (blk, blk), 1)
    uu = uu_ref[...]

    def body(step, carry):
        acc, run = carry
        j = i - step
        r0 = pl.multiple_of(j * blk, blk)
        kb = k_ref[pl.ds(r0, blk), :].astype(BF16)
        vb = v_ref[pl.ds(r0, blk), :].astype(BF16)
        z = _dot_nt(q, kb)
        kpos = j * blk + lane
        mask = (kpos < qpos) & (kpos >= pad)
        sp = jnp.log1p(jnp.exp(-jnp.abs(z)))
        ls_pos = jnp.minimum(z, 0.0) - sp
        lk = jnp.where(mask, ls_pos - z, 0.0)
        hi = lk.astype(BF16)
        lo = (lk - hi.astype(F32)).astype(BF16)
        later = _dot(jnp.concatenate([hi, lo], axis=1), uu) + run
        w = jnp.where(mask, jnp.exp(ls_pos + later), 0.0)
        acc = acc + _dot(w.astype(BF16), vb)
        run = run + jnp.sum(lk, axis=1, keepdims=True)
        return acc, run

    acc, _ = lax.fori_loop(0, i + 1, body,
                           (jnp.zeros((blk, HEAD_DIM), F32), jnp.zeros((blk, 1), F32)))
    o_ref[...] = (acc * _silu(z_ref[...])).astype(o_ref.dtype)


def _attention(plain, bsz, lp, pad, nh, off_q, off_k, off_v, off_z):
    blk = ATT_BLOCK
    nq = lp // blk
    j = lax.broadcasted_iota(jnp.int32, (2 * blk, blk), 0) % blk
    s = lax.broadcasted_iota(jnp.int32, (2 * blk, blk), 1)
    uu = (j > s).astype(BF16)
    bq, bk, bv, bz = (o // HEAD_DIM for o in (off_q, off_k, off_v, off_z))
    return pl.pallas_call(
        functools.partial(_attn_kernel, blk=blk, pad=pad, scale=HEAD_DIM ** -0.5),
        grid=(bsz, nh, nq),
        in_specs=[pl.BlockSpec((blk, HEAD_DIM), lambda b, h, i: (b * nq + i, bq + h)),
                  pl.BlockSpec((lp, HEAD_DIM), lambda b, h, i: (b, bk + h)),
                  pl.BlockSpec((lp, HEAD_DIM), lambda b, h, i: (b, bv + h)),
                  pl.BlockSpec((blk, HEAD_DIM), lambda b, h, i: (b * nq + i, bz + h)),
                  pl.BlockSpec((2 * blk, blk), lambda b, h, i: (0, 0))],
        out_specs=pl.BlockSpec((blk, HEAD_DIM), lambda b, h, i: (b * nq + i, h)),
        out_shape=jax.ShapeDtypeStruct((bsz * lp, nh * HEAD_DIM), BF16),
        compiler_params=_cparams("parallel", "parallel", "arbitrary"),
        name="sb_attention",
    )(plain, plain, plain, plain, uu)


def _gdn_kernel(q_ref, k_ref, v_ref, sm_ref, z_ref, ng_ref, ll3_ref, wx_ref, o_ref, *, lp, pad, nh):
    h = pl.program_id(1)
    cl = CHUNK
    c0 = pad // cl
    ll3 = ll3_ref[...]
    wx = wx_ref[...]
    ng = ng_ref[...]
    li = lax.broadcasted_iota(jnp.int32, (cl, cl), 0)
    si = lax.broadcasted_iota(jnp.int32, (cl, cl), 1)
    eye = (li == si).astype(F32)
    lane = lax.broadcasted_iota(jnp.int32, (cl, LANE), 1)
    if c0:
        o_ref[0:c0 * cl, :] = jnp.zeros((c0 * cl, HEAD_DIM), o_ref.dtype)

    def body(c, state):
        r0 = pl.multiple_of(c * cl, cl)
        q = q_ref[pl.ds(r0, cl), :]
        k = k_ref[pl.ds(r0, cl), :]
        v = v_ref[pl.ds(r0, cl), :]
        sm = sm_ref[pl.ds(r0, cl), :]
        beta = jnp.sum(jnp.where(lane == h, sm, 0.0), axis=1, keepdims=True)
        g = jnp.sum(jnp.where(lane == nh + h, sm, 0.0), axis=1, keepdims=True)
        gh, gm, gl = _split3(g * wx)
        sg = _dot(ll3, jnp.concatenate([gh, gm, gl], axis=0))
        seg = sg[0:cl, 0:cl]
        gcb = sg[0:cl, 2 * cl:4 * cl]
        glb = sg[cl:3 * cl, 2 * cl:4 * cl]
        eseg = jnp.exp(seg)
        dec_strict = jnp.where(li > si, eseg, 0.0)
        dec_incl = jnp.where(li >= si, eseg, 0.0)
        kb = k * beta
        m = _dot_nt(kb, k) * dec_strict
        p = -m
        t = eye + p
        for _ in range(5):
            p = _dot_hp(p, p)
            t = t + _dot_hp(t, p)
        egc = jnp.exp(gcb)
        u = _dot(t, v * beta)
        w = _dot(t, kb * egc)
        aqk = _dot_nt(q, k) * dec_incl
        qd = q * egc
        ke = k * jnp.exp(glb[0:cl, :] - gcb)
        v_new = u - _dot(w, state)
        o = _dot(qd, state) + _dot(aqk, v_new)
        state = state * jnp.exp(glb) + _dot_tn(ke, v_new)
        on = o * lax.rsqrt(jnp.mean(o * o, axis=1, keepdims=True) + RMS_EPS) * ng
        o_ref[pl.ds(r0, cl), :] = (on * _silu(z_ref[pl.ds(r0, cl), :])).astype(o_ref.dtype)
        return state

    lax.fori_loop(c0, lp // cl, body, jnp.zeros((HEAD_DIM, HEAD_DIM), F32))


def _gdn(gconv, small, plain, ng, bsz, lp, pad, nh, off_z):
    cl = CHUNK
    l = lax.broadcasted_iota(jnp.int32, (3 * cl, 3 * cl), 0)
    jj = lax.broadcasted_iota(jnp.int32, (3 * cl, 3 * cl), 1) % cl
    ll3 = ((l >= cl) | (jj <= l)).astype(BF16)
    j2 = lax.broadcasted_iota(jnp.int32, (cl, 4 * cl), 0)
    s2 = lax.broadcasted_iota(jnp.int32, (cl, 4 * cl), 1)
    wx = jnp.where(s2 < cl, (j2 > s2).astype(F32), (s2 >= 2 * cl).astype(F32))
    bz = off_z // HEAD_DIM
    seq = lambda off: pl.BlockSpec((lp, HEAD_DIM), lambda b, h: (b, off + h))
    const = lambda shp: pl.BlockSpec(shp, lambda b, h: (0, 0))
    return pl.pallas_call(
        functools.partial(_gdn_kernel, lp=lp, pad=pad, nh=nh),
        grid=(bsz, nh),
        in_specs=[seq(0), seq(nh), seq(2 * nh),
                  pl.BlockSpec((lp, LANE), lambda b, h: (b, 0)),
                  seq(bz), const((1, HEAD_DIM)), const((3 * cl, 3 * cl)), const((cl, 4 * cl))],
        out_specs=pl.BlockSpec((lp, HEAD_DIM), lambda b, h: (b, h)),
        out_shape=jax.ShapeDtypeStruct((bsz * lp, nh * HEAD_DIM), BF16),
        compiler_params=_cparams("parallel", "parallel"),
        name="gated_deltanet",
    )(gconv, gconv, gconv, small, plain, ng.reshape(1, HEAD_DIM), ll3, wx)


def _ssd_kernel(x_ref, b_ref, c_ref, sm_ref, z_ref, esel_ref, alog_ref, d_ref, ng_ref, ll3_ref, wt_ref,
                incl_ref, o_ref, s_ref, *, rows, hg):
    cl = CHUNK
    wd = hg * SSM_P

    @pl.when(pl.program_id(2) == 0)
    def _():
        s_ref[...] = jnp.zeros_like(s_ref)

    esel3 = esel_ref[...]
    a = -jnp.exp(alog_ref[...])
    dvec = d_ref[...]
    ng = ng_ref[...]
    ll3 = ll3_ref[...]
    wt = wt_ref[...]
    incl = incl_ref[...]
    lane = lax.broadcasted_iota(jnp.int32, (cl, LANE), 1)

    def body(c, carry):
        r0 = pl.multiple_of(c * cl, cl)
        x = x_ref[pl.ds(r0, cl), :]
        bm = b_ref[pl.ds(r0, cl), :]
        cm = c_ref[pl.ds(r0, cl), :]
        sh, smid, sl = _split3(sm_ref[pl.ds(r0, cl), :])
        dt = _dot(jnp.concatenate([sh, smid, sl], axis=1), esel3)
        la = dt * a
        xs = x * dt
        rh, rm, rl = _split3(jnp.concatenate([la * wt, la], axis=1))
        sg = _dot(ll3, jnp.concatenate([rh, rm, rl], axis=0))
        seg = sg[0:cl, 0:wd]
        cs = sg[0:cl, wd:2 * wd]
        tot = sg[cl:2 * cl, wd:2 * wd]
        decay = jnp.where(incl > 0.0, jnp.exp(seg), 0.0)
        scores = _dot_nt(cm, jnp.concatenate([bm] * hg, axis=0))
        mm = (scores * decay).astype(BF16)
        ys = []
        for pp in range(hg // 2):
            xp = xs[:, pp * LANE:(pp + 1) * LANE]
            bd = jnp.concatenate([jnp.where(lane < SSM_P, xp, 0.0), jnp.where(lane >= SSM_P, xp, 0.0)], axis=0)
            ys.append(_dot(mm[:, pp * LANE:(pp + 1) * LANE], bd.astype(BF16)))
        y_diag = jnp.concatenate(ys, axis=1)
        state = s_ref[...]
        y_off = _dot(cm, state) * jnp.exp(cs)
        s_ref[...] = state * jnp.exp(tot[0:1, :]) + _dot_tn(bm, xs * jnp.exp(tot - cs))
        y = (y_diag + y_off + dvec * x) * _silu(z_ref[pl.ds(r0, cl), :])
        o = y * lax.rsqrt(jnp.mean(y * y, axis=1, keepdims=True) + RMS_EPS) * ng
        o_ref[pl.ds(r0, cl), :] = o.astype(o_ref.dtype)
        return carry

    lax.fori_loop(0, rows // cl, body, 0)


def _ssd(sconv, small, plain, alog, dvec, ng, bsz, lp, n_gdn_heads, inner, off_z):
    cl = CHUNK
    ng_ = SSM_GROUPS
    wd = inner // ng_
    hg = wd // SSM_P
    ns = 2
    rows = lp // ns
    col = lax.broadcasted_iota(jnp.int32, (ng_, 3 * LANE, wd), 1) % LANE
    hh = lax.broadcasted_iota(jnp.int32, (ng_, 3 * LANE, wd), 2) // SSM_P
    gi = lax.broadcasted_iota(jnp.int32, (ng_, 3 * LANE, wd), 0)
    esel3 = (col == 2 * n_gdn_heads + gi * hg + hh).astype(BF16)
    l = lax.broadcasted_iota(jnp.int32, (2 * cl, 3 * cl), 0)
    jj = lax.broadcasted_iota(jnp.int32, (2 * cl, 3 * cl), 1) % cl
    ll3 = ((l >= cl) | (jj <= l)).astype(BF16)
    j2 = lax.broadcasted_iota(jnp.int32, (cl, wd), 0)
    s2 = lax.broadcasted_iota(jnp.int32, (cl, wd), 1) % cl
    wt = (j2 > s2).astype(F32)
    incl = (s2 <= j2).astype(F32)
    expand = lambda t: jnp.repeat(t.reshape(ng_, 1, hg), SSM_P, axis=2)
    bx = inner // LANE
    bz = off_z // wd
    rowblk = lambda w, f: pl.BlockSpec((rows, w), lambda b, g, s: (b * ns + s, f(g)))
    pergrp = lambda r: pl.BlockSpec((None, r, wd), lambda b, g, s: (g, 0, 0))
    const = lambda shp: pl.BlockSpec(shp, lambda b, g, s: (0, 0))
    return pl.pallas_call(
        functools.partial(_ssd_kernel, rows=rows, hg=hg),
        grid=(bsz, ng_, ns),
        in_specs=[rowblk(wd, lambda g: g), rowblk(SSM_STATE, lambda g: bx + g),
                  rowblk(SSM_STATE, lambda g: bx + ng_ + g), rowblk(LANE, lambda g: 0),
                  rowblk(wd, lambda g: bz + g),
                  pergrp(3 * LANE), pergrp(1), pergrp(1), pergrp(1),
                  const((2 * cl, 3 * cl)), const((cl, wd)), const((cl, wd))],
        out_specs=rowblk(wd, lambda g: g),
        out_shape=jax.ShapeDtypeStruct((bsz * lp, inner), BF16),
        scratch_shapes=[pltpu.VMEM((SSM_STATE, wd), F32)],
        compiler_params=_cparams("parallel", "parallel", "arbitrary"),
        name="ssd",
    )(sconv, sconv, sconv, small, plain, esel3, expand(alog), expand(dvec), ng.reshape(ng_, 1, wd),
      ll3, wt, incl)


def _out_kernel(oa_ref, ob_ref, oc_ref, ga_ref, gb_ref, gc_ref, h_ref, wa_ref, wb_ref, wc_ref, wo_ref,
                g_ref, hn_ref, xn_ref):
    merged = (_sigmoid(ga_ref[...]) * _dot(oa_ref[...], wa_ref[...])
              + _sigmoid(gb_ref[...]) * _dot(ob_ref[...], wb_ref[...])
              + _sigmoid(gc_ref[...]) * _dot(oc_ref[...], wc_ref[...]))
    hn = h_ref[...] + _dot(merged.astype(BF16), wo_ref[...])
    hn_ref[...] = hn
    y = hn * lax.rsqrt(jnp.mean(hn * hn, axis=-1, keepdims=True) + RMS_EPS) * g_ref[...]
    xn_ref[...] = y.astype(xn_ref.dtype)


def _out_stage(oa, ob, oc, plain, h, wa, wb, wc, wo, g_next, off_gates, xn_dtype):
    tp, d = h.shape
    tm = _pick(tp, (256, 128))
    bg = off_gates // d
    row = lambda w, cb=0: pl.BlockSpec((tm, w), lambda i: (i, cb))
    full = lambda a: pl.BlockSpec(a.shape, lambda i: (0, 0))
    return pl.pallas_call(
        _out_kernel,
        grid=(tp // tm,),
        in_specs=[row(oa.shape[1]), row(ob.shape[1]), row(oc.shape[1]),
                  row(d, bg), row(d, bg + 1), row(d, bg + 2), row(d),
                  full(wa), full(wb), full(wc), full(wo), pl.BlockSpec((1, d), lambda i: (0, 0))],
        out_specs=[row(d), row(d)],
        out_shape=[jax.ShapeDtypeStruct((tp, d), F32), jax.ShapeDtypeStruct((tp, d), xn_dtype)],
        compiler_params=_cparams("parallel"),
        name="merge_out",
    )(oa, ob, oc, plain, plain, plain, h, wa, wb, wc, wo, g_next.reshape(1, d))


def kernel(x, meta_tokens, norm_g, w_in, gdn_conv_w, gdn_a_log, gdn_dt_bias, gdn_norm_g, ssm_conv_w,
           ssm_conv_b, ssm_a_log, ssm_dt_bias, ssm_d, ssm_norm_g, w_branch_a, w_branch_b, w_branch_c,
           w_out, final_norm_g):
    bsz, seq, d = x.shape
    depth = norm_g.shape[0]
    sbw = w_branch_a.shape[1]
    vw = w_branch_b.shape[1]
    inner = w_branch_c.shape[1]
    hgdn = gdn_a_log.shape[1]
    hssm = ssm_a_log.shape[1]
    conv_ch = ssm_conv_w.shape[2]
    qkw = (gdn_conv_w.shape[2] - vw) // 2
    hsb = sbw // HEAD_DIM
    assert sbw == d and vw == d and qkw == hgdn * HEAD_DIM and vw == hgdn * HEAD_DIM
    assert inner == hssm * SSM_P and conv_ch == inner + 2 * SSM_GROUPS * SSM_STATE
    assert 2 * hgdn + hssm <= LANE and (inner // SSM_GROUPS) % (2 * SSM_P) == 0

    lreal = N_META + seq
    lp = -(-(lreal + CONV_K - 1) // ATT_BLOCK) * ATT_BLOCK
    pad = lp - lreal
    assert (lp // 2) % CHUNK == 0

    o_sb = 0
    o_gq = 4 * sbw
    o_gz = o_gq + 2 * qkw + vw
    o_gb = o_gz + vw
    o_ga = o_gb + hgdn
    o_sz = o_ga + hgdn
    o_sx = o_sz + inner
    o_dt = o_sx + conv_ch
    o_gt = o_dt + hssm
    assert w_in.shape[2] == o_gt + 3 * d
    w_plain = jnp.concatenate([w_in[:, :, o_sb:o_gq], w_in[:, :, o_gz:o_gb], w_in[:, :, o_gt:o_gt + 3 * d],
                               w_in[:, :, o_sz:o_sx]], axis=2).astype(BF16)
    p_gz = 4 * sbw
    p_gt = p_gz + vw
    p_sz = p_gt + 3 * d
    w_gconv = w_in[:, :, o_gq:o_gz].astype(BF16)
    w_sconv = w_in[:, :, o_sx:o_dt].astype(BF16)
    nsmall = 2 * hgdn + hssm
    w_small = jnp.concatenate([w_in[:, :, o_gb:o_sz], w_in[:, :, o_dt:o_gt],
                               jnp.zeros((depth, d, LANE - nsmall), F32)], axis=2).astype(BF16)
    zpad = lambda t: jnp.concatenate([t, jnp.zeros((depth, LANE - t.shape[1]), F32)], axis=1)
    kind = jnp.concatenate([jnp.zeros((hgdn,), F32), jnp.ones((hgdn,), F32), jnp.full((hssm,), 2.0, F32),
                            jnp.full((LANE - nsmall,), 3.0, F32)])
    small_par = jnp.stack([
        jnp.broadcast_to(kind, (depth, LANE)),
        zpad(jnp.concatenate([jnp.zeros((depth, hgdn), F32), gdn_a_log], axis=1)),
        zpad(jnp.concatenate([jnp.zeros((depth, hgdn), F32), gdn_dt_bias, ssm_dt_bias], axis=1)),
    ] + [jnp.zeros((depth, LANE), F32)] * 5, axis=1)
    wa, wb, wc, wo = (t.astype(BF16) for t in (w_branch_a, w_branch_b, w_branch_c, w_out))
    gconv_b = jnp.zeros((1, w_gconv.shape[2]), F32)

    meta = jnp.broadcast_to(meta_tokens.astype(x.dtype)[None], (bsz, N_META, d))
    h = jnp.concatenate([jnp.zeros((bsz, pad, d), x.dtype), meta, x], axis=1).reshape(bsz * lp, d)
    xn = _rmsnorm(h, norm_g[0], BF16)

    tn_plain = _pick(w_plain.shape[2], (512, 256))
    tn_g = _pick(qkw, (256, 128))
    tn_s = _pick(conv_ch, (256, 128))
    conv_specs = lambda tn: (pl.BlockSpec((CONV_K, tn), lambda b, j: (0, j)),
                             pl.BlockSpec((1, tn), lambda b, j: (0, j)))
    for layer in range(depth):
        plain = _inproj(xn, w_plain[layer], bsz, lp, tn=tn_plain, kernel=_inproj_plain_kernel, name="inproj_plain")
        gconv = _inproj(xn, w_gconv[layer], bsz, lp, tn=tn_g,
                        kernel=functools.partial(_inproj_conv_kernel, pad=pad, norm_blocks=qkw // tn_g,
                                                 q_scale=HEAD_DIM ** -0.5),
                        extra=(gdn_conv_w[layer], gconv_b), extra_specs=conv_specs(tn_g), name="inproj_gdn_conv")
        sconv = _inproj(xn, w_sconv[layer], bsz, lp, tn=tn_s,
                        kernel=functools.partial(_inproj_conv_kernel, pad=pad, norm_blocks=0, q_scale=1.0),
                        extra=(ssm_conv_w[layer], ssm_conv_b[layer].reshape(1, conv_ch)),
                        extra_specs=conv_specs(tn_s), name="inproj_ssm_conv")
        small = _inproj(xn, w_small[layer], bsz, lp, tn=LANE,
                        kernel=functools.partial(_inproj_small_kernel, pad=pad),
                        extra=(small_par[layer],), extra_specs=(pl.BlockSpec((8, LANE), lambda b, j: (0, 0)),),
                        name="inproj_small")
        oa = _attention(plain, bsz, lp, pad, hsb, 0, sbw, 2 * sbw, 3 * sbw)
        ob = _gdn(gconv, small, plain, gdn_norm_g[layer], bsz, lp, pad, hgdn, p_gz)
        oc = _ssd(sconv, small, plain, ssm_a_log[layer], ssm_d[layer], ssm_norm_g[layer], bsz, lp, hgdn,
                  inner, p_sz)
        last = layer == depth - 1
        g_next = final_norm_g if last else norm_g[layer + 1]
        h, xn = _out_stage(oa, ob, oc, plain, h, wa[layer], wb[layer], wc[layer], wo[layer], g_next, p_gt,
                           F32 if last else BF16)
    return xn.reshape(bsz, lp, d)[:, pad + N_META:]
```

```python
import functools

import jax
import jax.numpy as jnp
from jax import lax
from jax.experimental import pallas as pl
from jax.experimental.pallas import tpu as pltpu

F32 = jnp.float32
BF16 = jnp.bfloat16

N_META = 16
RMS_EPS = 1e-6
L2_EPS = 1e-6
CONV_K = 4
HEAD_DIM = 128
SSM_GROUPS = 2
SSM_STATE = 128
SSM_P = 64
CHUNK = 64
LANE = 128
ATT_BLOCK = 256
VMEM_LIMIT = 52 * 1024 * 1024
LOG2E = 1.4426950408889634
GDN_UNROLL = 6
ATT_HEADS = 4


def _cparams(*sem):
    return pltpu.CompilerParams(dimension_semantics=sem, vmem_limit_bytes=VMEM_LIMIT)


def _sigmoid(x):
    return 1.0 / (1.0 + jnp.exp(-x))


def _silu(x):
    return x * _sigmoid(x)


def _softplus(x):
    return jnp.maximum(x, 0.0) + jnp.log1p(jnp.exp(-jnp.abs(x)))


def _split3(x):
    hi = x.astype(BF16)
    r = x - hi.astype(F32)
    mid = r.astype(BF16)
    lo = (r - mid.astype(F32)).astype(BF16)
    return hi, mid, lo


def _dot(a, b):
    return jnp.dot(a, b, preferred_element_type=F32)


def _dot_nt(a, b):
    return lax.dot_general(a, b, (((1,), (1,)), ((), ())), preferred_element_type=F32)


def _dot_tn(a, b):
    return lax.dot_general(a, b, (((0,), (0,)), ((), ())), preferred_element_type=F32)


def _dot_hp(a, b):
    ah = a.astype(BF16)
    al = (a - ah.astype(F32)).astype(BF16)
    bh = b.astype(BF16)
    bl = (b - bh.astype(F32)).astype(BF16)
    return _dot(ah, bh) + (_dot(al, bh) + _dot(ah, bl))


def _pick(n, cands):
    for c in cands:
        if n % c == 0:
            return c
    raise ValueError(f"no tile for {n} in {cands}")


def _rmsnorm_kernel(h_ref, g_ref, o_ref):
    x = h_ref[...]
    y = x * lax.rsqrt(jnp.mean(x * x, axis=-1, keepdims=True) + RMS_EPS) * g_ref[...]
    o_ref[...] = y.astype(o_ref.dtype)


def _rmsnorm(h, g, out_dtype):
    tp, d = h.shape
    tm = _pick(tp, (512, 384, 256, 128))
    return pl.pallas_call(
        _rmsnorm_kernel,
        grid=(tp // tm,),
        in_specs=[pl.BlockSpec((tm, d), lambda i: (i, 0)), pl.BlockSpec((1, d), lambda i: (0, 0))],
        out_specs=pl.BlockSpec((tm, d), lambda i: (i, 0)),
        out_shape=jax.ShapeDtypeStruct((tp, d), out_dtype),
        compiler_params=_cparams("parallel"),
        name="rmsnorm",
    )(h, g.reshape(1, d))


def _inproj_plain_kernel(x_ref, w_ref, o_ref):
    o_ref[...] = _dot(x_ref[...], w_ref[...]).astype(o_ref.dtype)


def _inproj_conv_kernel(x_ref, w_ref, cw_ref, cb_ref, o_ref, *, pad, norm_blocks, q_scale):
    acc = _dot(x_ref[...], w_ref[...])
    cw = cw_ref[...]
    y = acc * cw[CONV_K - 1:CONV_K, :]
    for k in range(CONV_K - 1):
        y = y + pltpu.roll(acc, CONV_K - 1 - k, axis=0) * cw[k:k + 1, :]
    y = _silu(y + cb_ref[...])
    row = lax.broadcasted_iota(jnp.int32, y.shape, 0)
    y = jnp.where(row >= pad, y, 0.0)
    if norm_blocks:
        j = pl.program_id(1)
        for s in range(y.shape[1] // HEAD_DIM):
            ys = y[:, s * HEAD_DIM:(s + 1) * HEAD_DIM]
            rs = lax.rsqrt(jnp.sum(ys * ys, axis=1, keepdims=True) + L2_EPS)
            fac = jnp.where(j < norm_blocks, rs * q_scale, jnp.where(j < 2 * norm_blocks, rs, 1.0))
            o_ref[:, s * HEAD_DIM:(s + 1) * HEAD_DIM] = (ys * fac).astype(o_ref.dtype)
    else:
        o_ref[...] = y.astype(o_ref.dtype)


def _inproj_small_kernel(x_ref, w_ref, p_ref, o_ref, *, pad):
    v = _dot(x_ref[...], w_ref[...])
    kind = p_ref[0:1, :]
    alog = p_ref[1:2, :]
    bias = p_ref[2:3, :]
    sp = _softplus(v + bias)
    out = jnp.where(kind == 0.0, _sigmoid(v),
                    jnp.where(kind == 1.0, -jnp.exp(alog) * sp, jnp.where(kind == 2.0, sp, 0.0)))
    row = lax.broadcasted_iota(jnp.int32, out.shape, 0)
    o_ref[...] = jnp.where(row >= pad, out, 0.0)


def _inproj(xn, w, bsz, lp, *, tn, kernel, extra=(), extra_specs=(), out_dtype=F32, name):
    d = xn.shape[1]
    n = w.shape[1]
    return pl.pallas_call(
        kernel,
        grid=(bsz, n // tn),
        in_specs=[pl.BlockSpec((lp, d), lambda b, j: (b, 0)),
                  pl.BlockSpec((d, tn), lambda b, j: (0, j)), *extra_specs],
        out_specs=pl.BlockSpec((lp, tn), lambda b, j: (b, j)),
        out_shape=jax.ShapeDtypeStruct((bsz * lp, n), out_dtype),
        compiler_params=_cparams("parallel", "arbitrary"),
        name=name,
    )(xn, w, *extra)


def _attn_kernel(q_ref, k_ref, v_ref, z_ref, uu_ref, o_ref, *, blk, nhb, scale):
    i = pl.program_id(2)
    uu = uu_ref[...]
    causal = (lax.broadcasted_iota(jnp.int32, (blk, blk), 1) < lax.broadcasted_iota(jnp.int32, (blk, blk), 0))
    qs = [(q_ref[:, hh * HEAD_DIM:(hh + 1) * HEAD_DIM] * (scale * LOG2E)).astype(BF16) for hh in range(nhb)]

    def logits(hh, j):
        r0 = pl.multiple_of(j * blk, blk)
        kb = k_ref[pl.ds(r0, blk), hh * HEAD_DIM:(hh + 1) * HEAD_DIM].astype(BF16)
        return _dot_nt(qs[hh], kb)

    def log_terms(z, diag):
        sp = jnp.log2(1.0 + jnp.exp2(-jnp.abs(z)))
        ls_pos = jnp.minimum(z, 0.0) - sp
        lk = ls_pos - z
        if diag:
            lk = jnp.where(causal, lk, 0.0)
        hi = lk.astype(BF16)
        lo = (lk - hi.astype(F32)).astype(BF16)
        return ls_pos, lk, jnp.concatenate([hi, lo], axis=1)

    def weights(ls_pos, later, diag):
        w = jnp.exp2(ls_pos + later)
        if diag:
            w = jnp.where(causal, w, 0.0)
        return w.astype(BF16)

    def values(hh, j):
        r0 = pl.multiple_of(j * blk, blk)
        return v_ref[pl.ds(r0, blk), hh * HEAD_DIM:(hh + 1) * HEAD_DIM].astype(BF16)

    def tiles(j, acc, run, diag):
        heads = range(nhb)
        zs = [logits(hh, j) for hh in heads]
        lt = [log_terms(z, diag) for z in zs]
        later = [_dot(lt[hh][2], uu) + run[hh * blk:(hh + 1) * blk] for hh in heads]
        ws = [weights(lt[hh][0], later[hh], diag) for hh in heads]
        pv = [_dot(ws[hh], values(hh, j)) for hh in heads]
        rs = [jnp.sum(lt[hh][1], axis=1, keepdims=True) for hh in heads]
        return acc + jnp.concatenate(pv, axis=0), run + jnp.concatenate(rs, axis=0)

    carry = tiles(i, jnp.zeros((nhb * blk, HEAD_DIM), F32), jnp.zeros((nhb * blk, 1), F32), True)
    acc, _ = lax.fori_loop(1, i + 1, lambda step, c: tiles(i - step, *c, False), carry)
    for hh in range(nhb):
        sl = slice(hh * HEAD_DIM, (hh + 1) * HEAD_DIM)
        o_ref[:, sl] = (acc[hh * blk:(hh + 1) * blk] * _silu(z_ref[:, sl])).astype(o_ref.dtype)


def _attention(plain, bsz, lp, nh, off_q, off_k, off_v, off_z):
    blk = ATT_BLOCK
    nq = lp // blk
    nhb = next(n for n in (ATT_HEADS, 2, 1) if nh % n == 0)
    wd = nhb * HEAD_DIM
    j = lax.broadcasted_iota(jnp.int32, (2 * blk, blk), 0) % blk
    s = lax.broadcasted_iota(jnp.int32, (2 * blk, blk), 1)
    uu = (j > s).astype(BF16)
    bq, bk, bv, bz = (o // wd for o in (off_q, off_k, off_v, off_z))
    return pl.pallas_call(
        functools.partial(_attn_kernel, blk=blk, nhb=nhb, scale=HEAD_DIM ** -0.5),
        grid=(bsz, nh // nhb, nq),
        in_specs=[pl.BlockSpec((blk, wd), lambda b, h, i: (b * nq + i, bq + h)),
                  pl.BlockSpec((lp, wd), lambda b, h, i: (b, bk + h)),
                  pl.BlockSpec((lp, wd), lambda b, h, i: (b, bv + h)),
                  pl.BlockSpec((blk, wd), lambda b, h, i: (b * nq + i, bz + h)),
                  pl.BlockSpec((2 * blk, blk), lambda b, h, i: (0, 0))],
        out_specs=pl.BlockSpec((blk, wd), lambda b, h, i: (b * nq + i, h)),
        out_shape=jax.ShapeDtypeStruct((bsz * lp, nh * HEAD_DIM), BF16),
        compiler_params=_cparams("parallel", "parallel", "arbitrary"),
        name="sb_attention",
    )(plain, plain, plain, plain, uu)


def _gdn_kernel(q_ref, k_ref, v_ref, sm_ref, z_ref, ng_ref, ll3_ref, wx_ref, o_ref,
                a_s, b_s, qp_s, ob_s, eg_s, *, lp, pad, nh, unroll):
    h = pl.program_id(1)
    cl = CHUNK
    nc = lp // cl
    c0 = pad // cl
    c1 = nc - (nc - c0 + unroll - 1) // unroll * unroll
    assert c1 >= 0
    ll3 = ll3_ref[...]
    wx = wx_ref[...]
    li = lax.broadcasted_iota(jnp.int32, (cl, cl), 0)
    si = lax.broadcasted_iota(jnp.int32, (cl, cl), 1)
    eye = (li == si).astype(F32)
    lane = lax.broadcasted_iota(jnp.int32, (cl, LANE), 1)
    if c1:
        ob_s[0:c1 * cl, :] = jnp.zeros((c1 * cl, HEAD_DIM), F32)

    def pass1(i, carry):
        c = c1 + i * unroll
        cs = range(unroll)
        rows = [pl.ds(pl.multiple_of((c + r) * cl, cl), cl) for r in cs]
        q = [q_ref[rw, :] for rw in rows]
        k = [k_ref[rw, :] for rw in rows]
        v = [v_ref[rw, :] for rw in rows]
        sm = [sm_ref[rw, :] for rw in rows]
        beta = [jnp.sum(jnp.where(lane == h, x, 0.0), axis=1, keepdims=True) for x in sm]
        g = [jnp.sum(jnp.where(lane == nh + h, x, 0.0), axis=1, keepdims=True) for x in sm]
        sg = [_dot(ll3, jnp.concatenate(_split3(x * wx), axis=0)) for x in g]
        eseg = [jnp.exp(x[0:cl, 0:cl]) for x in sg]
        gcb = [x[0:cl, 2 * cl:4 * cl] for x in sg]
        glb = [x[cl:2 * cl, 2 * cl:4 * cl] for x in sg]
        kb = [k[r] * beta[r] for r in cs]
        kk = [_dot_nt(kb[r], k[r]) for r in cs]
        p = [-(kk[r] * jnp.where(li > si, eseg[r], 0.0)) for r in cs]
        t = [eye + x for x in p]
        for _ in range(5):
            p = [_dot_hp(x, x) for x in p]
            t = [t[r] + _dot_hp(t[r], p[r]) for r in cs]
        egc = [jnp.exp(x) for x in gcb]
        u = [_dot(t[r], v[r] * beta[r]) for r in cs]
        w = [_dot(t[r], kb[r] * egc[r]) for r in cs]
        aqk = [_dot_nt(q[r], k[r]) * jnp.where(li >= si, eseg[r], 0.0) for r in cs]
        ke = [k[r] * jnp.exp(glb[r] - gcb[r]) for r in cs]
        outs = ([_dot_tn(ke[r], w[r]).astype(a_s.dtype) for r in cs], [_dot_tn(ke[r], u[r]) for r in cs],
                [(q[r] * egc[r] - _dot(aqk[r], w[r])).astype(qp_s.dtype) for r in cs],
                [_dot(aqk[r], u[r]) for r in cs], [jnp.exp(x[0:8, :]) for x in glb])
        for ref, nr, vals in zip((a_s, b_s, qp_s, ob_s, eg_s), (HEAD_DIM, HEAD_DIM, cl, cl, 8), outs):
            ref[pl.ds(pl.multiple_of(c * nr, nr), unroll * nr), :] = jnp.concatenate(vals, axis=0)
        return carry

    lax.fori_loop(0, (nc - c1) // unroll, pass1, 0)

    def pass2(c, state):
        r0 = pl.multiple_of(c * cl, cl)
        rk = pl.ds(pl.multiple_of(c * HEAD_DIM, HEAD_DIM), HEAD_DIM)
        lhs = jnp.concatenate([a_s[rk, :], qp_s[pl.ds(r0, cl), :]], axis=0)
        mm = _dot(lhs, state.astype(BF16))
        ob_s[pl.ds(r0, cl), :] = ob_s[pl.ds(r0, cl), :] + mm[HEAD_DIM:, :]
        eg = eg_s[pl.ds(pl.multiple_of(c * 8, 8), 8), :]
        return state * eg[0:1, :] - mm[0:HEAD_DIM, :] + b_s[rk, :]

    lax.fori_loop(c0, nc, pass2, jnp.zeros((HEAD_DIM, HEAD_DIM), F32))

    o = ob_s[...]
    on = o * lax.rsqrt(jnp.mean(o * o, axis=1, keepdims=True) + RMS_EPS) * ng_ref[...]
    o_ref[...] = (on * _silu(z_ref[...])).astype(o_ref.dtype)


def _gdn(gconv, small, plain, ng, bsz, lp, pad, nh, off_z):
    cl = CHUNK
    nc = lp // cl
    n = nc - pad // cl
    unroll = next(u for u in range(min(GDN_UNROLL, nc), 0, -1) if -(-n // u) * u <= nc)
    l = lax.broadcasted_iota(jnp.int32, (2 * cl, 3 * cl), 0)
    jj = lax.broadcasted_iota(jnp.int32, (2 * cl, 3 * cl), 1) % cl
    ll3 = ((l >= cl) | (jj <= l)).astype(BF16)
    j2 = lax.broadcasted_iota(jnp.int32, (cl, 4 * cl), 0)
    s2 = lax.broadcasted_iota(jnp.int32, (cl, 4 * cl), 1)
    wx = jnp.where(s2 < cl, (j2 > s2).astype(F32), (s2 >= 2 * cl).astype(F32))
    bz = off_z // HEAD_DIM
    seq = lambda off: pl.BlockSpec((lp, HEAD_DIM), lambda b, h: (b, off + h))
    const = lambda shp: pl.BlockSpec(shp, lambda b, h: (0, 0))
    return pl.pallas_call(
        functools.partial(_gdn_kernel, lp=lp, pad=pad, nh=nh, unroll=unroll),
        grid=(bsz, nh),
        in_specs=[seq(0), seq(nh), seq(2 * nh),
                  pl.BlockSpec((lp, LANE), lambda b, h: (b, 0)),
                  seq(bz), const((1, HEAD_DIM)), const((2 * cl, 3 * cl)), const((cl, 4 * cl))],
        out_specs=pl.BlockSpec((lp, HEAD_DIM), lambda b, h: (b, h)),
        out_shape=jax.ShapeDtypeStruct((bsz * lp, nh * HEAD_DIM), BF16),
        scratch_shapes=[pltpu.VMEM((nc * HEAD_DIM, HEAD_DIM), BF16), pltpu.VMEM((nc * HEAD_DIM, HEAD_DIM), F32),
                        pltpu.VMEM((lp, HEAD_DIM), BF16), pltpu.VMEM((lp, HEAD_DIM), F32),
                        pltpu.VMEM((nc * 8, HEAD_DIM), F32)],
        compiler_params=_cparams("parallel", "parallel"),
        name="gated_deltanet",
    )(gconv, gconv, gconv, small, plain, ng.reshape(1, HEAD_DIM), ll3, wx)


def _ssd_kernel(x_ref, b_ref, c_ref, sm_ref, z_ref, esel_ref, alog_ref, d_ref, ng_ref, ll3_ref, wt_ref,
                incl_ref, o_ref, s_ref, *, rows, hg):
    cl = CHUNK
    wd = hg * SSM_P

    @pl.when(pl.program_id(2) == 0)
    def _():
        s_ref[...] = jnp.zeros_like(s_ref)

    esel3 = esel_ref[...]
    a = -jnp.exp(alog_ref[...])
    dvec = d_ref[...]
    ng = ng_ref[...]
    ll3 = ll3_ref[...]
    wt = wt_ref[...]
    incl = incl_ref[...]
    lane = lax.broadcasted_iota(jnp.int32, (cl, LANE), 1)

    def body(c, carry):
        r0 = pl.multiple_of(c * cl, cl)
        x = x_ref[pl.ds(r0, cl), :]
        bm = b_ref[pl.ds(r0, cl), :]
        cm = c_ref[pl.ds(r0, cl), :]
        sh, smid, sl = _split3(sm_ref[pl.ds(r0, cl), :])
        dt = _dot(jnp.concatenate([sh, smid, sl], axis=1), esel3)
        la = dt * a
        xs = x * dt
        rh, rm, rl = _split3(jnp.concatenate([la * wt, la], axis=1))
        sg = _dot(ll3, jnp.concatenate([rh, rm, rl], axis=0))
        seg = sg[0:cl, 0:wd]
        cs = sg[0:cl, wd:2 * wd]
        tot = sg[cl:2 * cl, wd:2 * wd]
        decay = jnp.where(incl > 0.0, jnp.exp(seg), 0.0)
        scores = _dot_nt(cm, jnp.concatenate([bm] * hg, axis=0))
        mm = (scores * decay).astype(BF16)
        ys = []
        for pp in range(hg // 2):
            xp = xs[:, pp * LANE:(pp + 1) * LANE]
            bd = jnp.concatenate([jnp.where(lane < SSM_P, xp, 0.0), jnp.where(lane >= SSM_P, xp, 0.0)], axis=0)
            ys.append(_dot(mm[:, pp * LANE:(pp + 1) * LANE], bd.astype(BF16)))
        y_diag = jnp.concatenate(ys, axis=1)
        state = s_ref[...]
        y_off = _dot(cm, state) * jnp.exp(cs)
        s_ref[...] = state * jnp.exp(tot[0:1, :]) + _dot_tn(bm, xs * jnp.exp(tot - cs))
        y = (y_diag + y_off + dvec * x) * _silu(z_ref[pl.ds(r0, cl), :])
        o = y * lax.rsqrt(jnp.mean(y * y, axis=1, keepdims=True) + RMS_EPS) * ng
        o_ref[pl.ds(r0, cl), :] = o.astype(o_ref.dtype)
        return carry

    lax.fori_loop(0, rows // cl, body, 0)


def _ssd(sconv, small, plain, alog, dvec, ng, bsz, lp, n_gdn_heads, inner, off_z):
    cl = CHUNK
    ng_ = SSM_GROUPS
    wd = inner // ng_
    hg = wd // SSM_P
    ns = 2
    rows = lp // ns
    col = lax.broadcasted_iota(jnp.int32, (ng_, 3 * LANE, wd), 1) % LANE
    hh = lax.broadcasted_iota(jnp.int32, (ng_, 3 * LANE, wd), 2) // SSM_P
    gi = lax.broadcasted_iota(jnp.int32, (ng_, 3 * LANE, wd), 0)
    esel3 = (col == 2 * n_gdn_heads + gi * hg + hh).astype(BF16)
    l = lax.broadcasted_iota(jnp.int32, (2 * cl, 3 * cl), 0)
    jj = lax.broadcasted_iota(jnp.int32, (2 * cl, 3 * cl), 1) % cl
    ll3 = ((l >= cl) | (jj <= l)).astype(BF16)
    j2 = lax.broadcasted_iota(jnp.int32, (cl, wd), 0)
    s2 = lax.broadcasted_iota(jnp.int32, (cl, wd), 1) % cl
    wt = (j2 > s2).astype(F32)
    incl = (s2 <= j2).astype(F32)
    expand = lambda t: jnp.repeat(t.reshape(ng_, 1, hg), SSM_P, axis=2)
    bx = inner // LANE
    bz = off_z // wd
    rowblk = lambda w, f: pl.BlockSpec((rows, w), lambda b, g, s: (b * ns + s, f(g)))
    pergrp = lambda r: pl.BlockSpec((None, r, wd), lambda b, g, s: (g, 0, 0))
    const = lambda shp: pl.BlockSpec(shp, lambda b, g, s: (0, 0))
    return pl.pallas_call(
        functools.partial(_ssd_kernel, rows=rows, hg=hg),
        grid=(bsz, ng_, ns),
        in_specs=[rowblk(wd, lambda g: g), rowblk(SSM_STATE, lambda g: bx + g),
                  rowblk(SSM_STATE, lambda g: bx + ng_ + g), rowblk(LANE, lambda g: 0),
                  rowblk(wd, lambda g: bz + g),
                  pergrp(3 * LANE), pergrp(1), pergrp(1), pergrp(1),
                  const((2 * cl, 3 * cl)), const((cl, wd)), const((cl, wd))],
        out_specs=rowblk(wd, lambda g: g),
        out_shape=jax.ShapeDtypeStruct((bsz * lp, inner), BF16),
        scratch_shapes=[pltpu.VMEM((SSM_STATE, wd), F32)],
        compiler_params=_cparams("parallel", "parallel", "arbitrary"),
        name="ssd",
    )(sconv, sconv, sconv, small, plain, esel3, expand(alog), expand(dvec), ng.reshape(ng_, 1, wd),
      ll3, wt, incl)


def _out_kernel(oa_ref, ob_ref, oc_ref, ga_ref, gb_ref, gc_ref, h_ref, wa_ref, wb_ref, wc_ref, wo_ref,
                g_ref, hn_ref, xn_ref):
    merged = (_sigmoid(ga_ref[...]) * _dot(oa_ref[...], wa_ref[...])
              + _sigmoid(gb_ref[...]) * _dot(ob_ref[...], wb_ref[...])
              + _sigmoid(gc_ref[...]) * _dot(oc_ref[...], wc_ref[...]))
    hn = h_ref[...] + _dot(merged.astype(BF16), wo_ref[...])
    hn_ref[...] = hn
    y = hn * lax.rsqrt(jnp.mean(hn * hn, axis=-1, keepdims=True) + RMS_EPS) * g_ref[...]
    xn_ref[...] = y.astype(xn_ref.dtype)


def _out_stage(oa, ob, oc, plain, h, wa, wb, wc, wo, g_next, off_gates, xn_dtype):
    tp, d = h.shape
    tm = _pick(tp, (256, 128))
    bg = off_gates // d
    row = lambda w, cb=0: pl.BlockSpec((tm, w), lambda i: (i, cb))
    full = lambda a: pl.BlockSpec(a.shape, lambda i: (0, 0))
    return pl.pallas_call(
        _out_kernel,
        grid=(tp // tm,),
        in_specs=[row(oa.shape[1]), row(ob.shape[1]), row(oc.shape[1]),
                  row(d, bg), row(d, bg + 1), row(d, bg + 2), row(d),
                  full(wa), full(wb), full(wc), full(wo), pl.BlockSpec((1, d), lambda i: (0, 0))],
        out_specs=[row(d), row(d)],
        out_shape=[jax.ShapeDtypeStruct((tp, d), F32), jax.ShapeDtypeStruct((tp, d), xn_dtype)],
        compiler_params=_cparams("parallel"),
        name="merge_out",
    )(oa, ob, oc, plain, plain, plain, h, wa, wb, wc, wo, g_next.reshape(1, d))


def kernel(x, meta_tokens, norm_g, w_in, gdn_conv_w, gdn_a_log, gdn_dt_bias, gdn_norm_g, ssm_conv_w,
           ssm_conv_b, ssm_a_log, ssm_dt_bias, ssm_d, ssm_norm_g, w_branch_a, w_branch_b, w_branch_c,
           w_out, final_norm_g):
    bsz, seq, d = x.shape
    depth = norm_g.shape[0]
    sbw = w_branch_a.shape[1]
    vw = w_branch_b.shape[1]
    inner = w_branch_c.shape[1]
    hgdn = gdn_a_log.shape[1]
    hssm = ssm_a_log.shape[1]
    conv_ch = ssm_conv_w.shape[2]
    qkw = (gdn_conv_w.shape[2] - vw) // 2
    hsb = sbw // HEAD_DIM
    assert sbw == d and vw == d and qkw == hgdn * HEAD_DIM and vw == hgdn * HEAD_DIM
    assert inner == hssm * SSM_P and conv_ch == inner + 2 * SSM_GROUPS * SSM_STATE
    assert 2 * hgdn + hssm <= LANE and (inner // SSM_GROUPS) % (2 * SSM_P) == 0

    lreal = N_META + seq
    lp = -(-(lreal + CONV_K - 1) // ATT_BLOCK) * ATT_BLOCK
    pad = lp - lreal
    assert (lp // 2) % CHUNK == 0

    o_sb = 0
    o_gq = 4 * sbw
    o_gz = o_gq + 2 * qkw + vw
    o_gb = o_gz + vw
    o_ga = o_gb + hgdn
    o_sz = o_ga + hgdn
    o_sx = o_sz + inner
    o_dt = o_sx + conv_ch
    o_gt = o_dt + hssm
    assert w_in.shape[2] == o_gt + 3 * d
    w_plain = jnp.concatenate([w_in[:, :, o_sb:o_gq], w_in[:, :, o_gz:o_gb], w_in[:, :, o_gt:o_gt + 3 * d],
                               w_in[:, :, o_sz:o_sx]], axis=2).astype(BF16)
    p_gz = 4 * sbw
    p_gt = p_gz + vw
    p_sz = p_gt + 3 * d
    w_gconv = w_in[:, :, o_gq:o_gz].astype(BF16)
    w_sconv = w_in[:, :, o_sx:o_dt].astype(BF16)
    nsmall = 2 * hgdn + hssm
    w_small = jnp.concatenate([w_in[:, :, o_gb:o_sz], w_in[:, :, o_dt:o_gt],
                               jnp.zeros((depth, d, LANE - nsmall), F32)], axis=2).astype(BF16)
    zpad = lambda t: jnp.concatenate([t, jnp.zeros((depth, LANE - t.shape[1]), F32)], axis=1)
    kind = jnp.concatenate([jnp.zeros((hgdn,), F32), jnp.ones((hgdn,), F32), jnp.full((hssm,), 2.0, F32),
                            jnp.full((LANE - nsmall,), 3.0, F32)])
    small_par = jnp.stack([
        jnp.broadcast_to(kind, (depth, LANE)),
        zpad(jnp.concatenate([jnp.zeros((depth, hgdn), F32), gdn_a_log], axis=1)),
        zpad(jnp.concatenate([jnp.zeros((depth, hgdn), F32), gdn_dt_bias, ssm_dt_bias], axis=1)),
    ] + [jnp.zeros((depth, LANE), F32)] * 5, axis=1)
    wa, wb, wc, wo = (t.astype(BF16) for t in (w_branch_a, w_branch_b, w_branch_c, w_out))
    gconv_b = jnp.zeros((1, w_gconv.shape[2]), F32)

    meta = jnp.broadcast_to(meta_tokens.astype(x.dtype)[None], (bsz, N_META, d))
    h = jnp.concatenate([jnp.zeros((bsz, pad, d), x.dtype), meta, x], axis=1).reshape(bsz * lp, d)
    xn = _rmsnorm(h, norm_g[0], BF16)

    tn_plain = _pick(w_plain.shape[2], (512, 256))
    tn_g = _pick(qkw, (256, 128))
    tn_s = _pick(conv_ch, (256, 128))
    conv_specs = lambda tn: (pl.BlockSpec((CONV_K, tn), lambda b, j: (0, j)),
                             pl.BlockSpec((1, tn), lambda b, j: (0, j)))
    for layer in range(depth):
        plain = _inproj(xn, w_plain[layer], bsz, lp, tn=tn_plain, kernel=_inproj_plain_kernel, name="inproj_plain")
        gconv = _inproj(xn, w_gconv[layer], bsz, lp, tn=tn_g,
                        kernel=functools.partial(_inproj_conv_kernel, pad=pad, norm_blocks=qkw // tn_g,
                                                 q_scale=HEAD_DIM ** -0.5),
                        extra=(gdn_conv_w[layer], gconv_b), extra_specs=conv_specs(tn_g), name="inproj_gdn_conv")
        sconv = _inproj(xn, w_sconv[layer], bsz, lp, tn=tn_s,
                        kernel=functools.partial(_inproj_conv_kernel, pad=pad, norm_blocks=0, q_scale=1.0),
                        extra=(ssm_conv_w[layer], ssm_conv_b[layer].reshape(1, conv_ch)),
                        extra_specs=conv_specs(tn_s), name="inproj_ssm_conv")
        small = _inproj(xn, w_small[layer], bsz, lp, tn=LANE,
                        kernel=functools.partial(_inproj_small_kernel, pad=pad),
                        extra=(small_par[layer],), extra_specs=(pl.BlockSpec((8, LANE), lambda b, j: (0, 0)),),
                        name="inproj_small")
        oa = _attention(plain, bsz, lp, hsb, 0, sbw, 2 * sbw, 3 * sbw)
        ob = _gdn(gconv, small, plain, gdn_norm_g[layer], bsz, lp, pad, hgdn, p_gz)
        oc = _ssd(sconv, small, plain, ssm_a_log[layer], ssm_d[layer], ssm_norm_g[layer], bsz, lp, hgdn,
                  inner, p_sz)
        last = layer == depth - 1
        g_next = final_norm_g if last else norm_g[layer + 1]
        h, xn = _out_stage(oa, ob, oc, plain, h, wa[layer], wb[layer], wc[layer], wo[layer], g_next, p_gt,
                           F32 if last else BF16)
    return xn.reshape(bsz, lp, d)[:, pad + N_META:]
```

```python
import functools

import jax
import jax.numpy as jnp
from jax import lax
from jax.experimental import pallas as pl
from jax.experimental.pallas import tpu as pltpu

F32 = jnp.float32
BF16 = jnp.bfloat16

N_META = 16
RMS_EPS = 1e-6
L2_EPS = 1e-6
CONV_K = 4
HEAD_DIM = 128
SSM_GROUPS = 2
SSM_STATE = 128
SSM_P = 64
CHUNK = 64
LANE = 128
ATT_BLOCK = 256
VMEM_LIMIT = 52 * 1024 * 1024
LOG2E = 1.4426950408889634
GDN_UNROLL = 12
SSD_SPLIT = 1
GDN_HEADS = 2
ATT_HEADS = 4
ATT_LOG_TERMS = 1


def _cparams(*sem):
    return pltpu.CompilerParams(dimension_semantics=sem, vmem_limit_bytes=VMEM_LIMIT)


def _sigmoid(x):
    return 1.0 / (1.0 + jnp.exp(-x))


def _silu(x):
    return x * _sigmoid(x)


def _softplus(x):
    return jnp.maximum(x, 0.0) + jnp.log1p(jnp.exp(-jnp.abs(x)))


def _split3(x):
    hi = x.astype(BF16)
    r = x - hi.astype(F32)
    mid = r.astype(BF16)
    lo = (r - mid.astype(F32)).astype(BF16)
    return hi, mid, lo


def _dot(a, b):
    return jnp.dot(a, b, preferred_element_type=F32)


def _dot_nt(a, b):
    return lax.dot_general(a, b, (((1,), (1,)), ((), ())), preferred_element_type=F32)


def _dot_tn(a, b):
    return lax.dot_general(a, b, (((0,), (0,)), ((), ())), preferred_element_type=F32)


def _dot_hp(a, b):
    ah = a.astype(BF16)
    al = (a - ah.astype(F32)).astype(BF16)
    bh = b.astype(BF16)
    bl = (b - bh.astype(F32)).astype(BF16)
    return _dot(ah, bh) + (_dot(al, bh) + _dot(ah, bl))


def _pick(n, cands):
    for c in cands:
        if n % c == 0:
            return c
    raise ValueError(f"no tile for {n} in {cands}")


def _rmsnorm_kernel(h_ref, g_ref, o_ref):
    x = h_ref[...]
    y = x * lax.rsqrt(jnp.mean(x * x, axis=-1, keepdims=True) + RMS_EPS) * g_ref[...]
    o_ref[...] = y.astype(o_ref.dtype)


def _rmsnorm(h, g, out_dtype):
    tp, d = h.shape
    tm = _pick(tp, (512, 384, 256, 128))
    return pl.pallas_call(
        _rmsnorm_kernel,
        grid=(tp // tm,),
        in_specs=[pl.BlockSpec((tm, d), lambda i: (i, 0)), pl.BlockSpec((1, d), lambda i: (0, 0))],
        out_specs=pl.BlockSpec((tm, d), lambda i: (i, 0)),
        out_shape=jax.ShapeDtypeStruct((tp, d), out_dtype),
        compiler_params=_cparams("parallel"),
        name="rmsnorm",
    )(h, g.reshape(1, d))


def _inproj_plain_kernel(x_ref, w_ref, o_ref):
    o_ref[...] = _dot(x_ref[...], w_ref[...]).astype(o_ref.dtype)


def _inproj_conv_kernel(x_ref, w_ref, cw_ref, cb_ref, o_ref, *, pad, norm_blocks, q_scale):
    acc = _dot(x_ref[...], w_ref[...])
    cw = cw_ref[...]
    y = acc * cw[CONV_K - 1:CONV_K, :]
    for k in range(CONV_K - 1):
        y = y + pltpu.roll(acc, CONV_K - 1 - k, axis=0) * cw[k:k + 1, :]
    y = _silu(y + cb_ref[...])
    row = lax.broadcasted_iota(jnp.int32, y.shape, 0)
    y = jnp.where(row >= pad, y, 0.0)
    if norm_blocks:
        j = pl.program_id(1)
        for s in range(y.shape[1] // HEAD_DIM):
            ys = y[:, s * HEAD_DIM:(s + 1) * HEAD_DIM]
            rs = lax.rsqrt(jnp.sum(ys * ys, axis=1, keepdims=True) + L2_EPS)
            fac = jnp.where(j < norm_blocks, rs * q_scale, jnp.where(j < 2 * norm_blocks, rs, 1.0))
            o_ref[:, s * HEAD_DIM:(s + 1) * HEAD_DIM] = (ys * fac).astype(o_ref.dtype)
    else:
        o_ref[...] = y.astype(o_ref.dtype)


def _inproj_small_kernel(x_ref, w_ref, p_ref, o_ref, *, pad):
    v = _dot(x_ref[...], w_ref[...])
    kind = p_ref[0:1, :]
    alog = p_ref[1:2, :]
    bias = p_ref[2:3, :]
    sp = _softplus(v + bias)
    out = jnp.where(kind == 0.0, _sigmoid(v),
                    jnp.where(kind == 1.0, -jnp.exp(alog) * sp, jnp.where(kind == 2.0, sp, 0.0)))
    row = lax.broadcasted_iota(jnp.int32, out.shape, 0)
    o_ref[...] = jnp.where(row >= pad, out, 0.0)


def _inproj(xn, w, bsz, lp, *, tn, kernel, extra=(), extra_specs=(), out_dtype=F32, name):
    d = xn.shape[1]
    n = w.shape[1]
    return pl.pallas_call(
        kernel,
        grid=(bsz, n // tn),
        in_specs=[pl.BlockSpec((lp, d), lambda b, j: (b, 0)),
                  pl.BlockSpec((d, tn), lambda b, j: (0, j)), *extra_specs],
        out_specs=pl.BlockSpec((lp, tn), lambda b, j: (b, j)),
        out_shape=jax.ShapeDtypeStruct((bsz * lp, n), out_dtype),
        compiler_params=_cparams("parallel", "arbitrary"),
        name=name,
    )(xn, w, *extra)


def _attn_kernel(q_ref, k_ref, v_ref, z_ref, uu_ref, o_ref, *, blk, nhb, scale):
    i = pl.program_id(2)
    uu = uu_ref[...]
    causal = (lax.broadcasted_iota(jnp.int32, (blk, blk), 1) < lax.broadcasted_iota(jnp.int32, (blk, blk), 0))
    qs = [(q_ref[:, hh * HEAD_DIM:(hh + 1) * HEAD_DIM] * (scale * LOG2E)).astype(BF16) for hh in range(nhb)]

    def logits(hh, j):
        r0 = pl.multiple_of(j * blk, blk)
        kb = k_ref[pl.ds(r0, blk), hh * HEAD_DIM:(hh + 1) * HEAD_DIM].astype(BF16)
        return _dot_nt(qs[hh], kb)

    def log_terms(z, diag):
        sp = jnp.log2(1.0 + jnp.exp2(-jnp.abs(z)))
        ls_pos = jnp.minimum(z, 0.0) - sp
        lk = ls_pos - z
        if diag:
            lk = jnp.where(causal, lk, 0.0)
        terms, resid = [], lk
        for t in range(ATT_LOG_TERMS):
            terms.append(resid.astype(BF16))
            if t + 1 < ATT_LOG_TERMS:
                resid = resid - terms[-1].astype(F32)
        return ls_pos, lk, jnp.concatenate(terms, axis=1) if len(terms) > 1 else terms[0]

    def weights(ls_pos, later, diag):
        w = jnp.exp2(ls_pos + later)
        if diag:
            w = jnp.where(causal, w, 0.0)
        return w.astype(BF16)

    def values(hh, j):
        r0 = pl.multiple_of(j * blk, blk)
        return v_ref[pl.ds(r0, blk), hh * HEAD_DIM:(hh + 1) * HEAD_DIM].astype(BF16)

    def tiles(j, acc, run, diag):
        heads = range(nhb)
        zs = [logits(hh, j) for hh in heads]
        lt = [log_terms(z, diag) for z in zs]
        later = [_dot(lt[hh][2], uu) + run[hh * blk:(hh + 1) * blk] for hh in heads]
        ws = [weights(lt[hh][0], later[hh], diag) for hh in heads]
        pv = [_dot(ws[hh], values(hh, j)) for hh in heads]
        rs = [jnp.sum(lt[hh][1], axis=1, keepdims=True) for hh in heads]
        return acc + jnp.concatenate(pv, axis=0), run + jnp.concatenate(rs, axis=0)

    carry = tiles(i, jnp.zeros((nhb * blk, HEAD_DIM), F32), jnp.zeros((nhb * blk, 1), F32), True)
    acc, _ = lax.fori_loop(1, i + 1, lambda step, c: tiles(i - step, *c, False), carry)
    for hh in range(nhb):
        sl = slice(hh * HEAD_DIM, (hh + 1) * HEAD_DIM)
        o_ref[:, sl] = (acc[hh * blk:(hh + 1) * blk] * _silu(z_ref[:, sl])).astype(o_ref.dtype)


def _attention(plain, bsz, lp, nh, off_q, off_k, off_v, off_z):
    blk = ATT_BLOCK
    nq = lp // blk
    nhb = next(n for n in (ATT_HEADS, 2, 1) if nh % n == 0)
    wd = nhb * HEAD_DIM
    j = lax.broadcasted_iota(jnp.int32, (ATT_LOG_TERMS * blk, blk), 0) % blk
    s = lax.broadcasted_iota(jnp.int32, (ATT_LOG_TERMS * blk, blk), 1)
    uu = (j > s).astype(BF16)
    bq, bk, bv, bz = (o // wd for o in (off_q, off_k, off_v, off_z))
    return pl.pallas_call(
        functools.partial(_attn_kernel, blk=blk, nhb=nhb, scale=HEAD_DIM ** -0.5),
        grid=(bsz, nh // nhb, nq),
        in_specs=[pl.BlockSpec((blk, wd), lambda b, h, i: (b * nq + i, bq + h)),
                  pl.BlockSpec((lp, wd), lambda b, h, i: (b, bk + h)),
                  pl.BlockSpec((lp, wd), lambda b, h, i: (b, bv + h)),
                  pl.BlockSpec((blk, wd), lambda b, h, i: (b * nq + i, bz + h)),
                  pl.BlockSpec((ATT_LOG_TERMS * blk, blk), lambda b, h, i: (0, 0))],
        out_specs=pl.BlockSpec((blk, wd), lambda b, h, i: (b * nq + i, h)),
        out_shape=jax.ShapeDtypeStruct((bsz * lp, nh * HEAD_DIM), BF16),
        compiler_params=_cparams("parallel", "parallel", "arbitrary"),
        name="sb_attention",
    )(plain, plain, plain, plain, uu)


def _gdn_kernel(q_ref, k_ref, v_ref, sm_ref, z_ref, ng_ref, ll3_ref, wx_ref, o_ref,
                a_s, b_s, qp_s, ob_s, eg_s, *, lp, pad, nh, nhb, unroll):
    h0 = pl.program_id(1) * nhb
    cl = CHUNK
    nc = lp // cl
    c0 = pad // cl
    c1 = nc - (nc - c0 + unroll - 1) // unroll * unroll
    assert c1 >= 0
    ll3 = ll3_ref[...]
    wx = (wx_ref[0:cl, :], wx_ref[cl:2 * cl, :])
    li = lax.broadcasted_iota(jnp.int32, (cl, 2 * cl), 0)
    lanew = lax.broadcasted_iota(jnp.int32, (cl, 2 * cl), 1)
    first = lanew < cl
    si = lanew & (cl - 1)
    eye = (li == si).astype(F32)
    lane = lax.broadcasted_iota(jnp.int32, (cl, LANE), 1)
    zero_b = jnp.zeros((cl, 2 * cl), BF16)
    zero_h = jnp.zeros((cl, HEAD_DIM), BF16)
    if c1:
        ob_s[:, 0:c1 * cl, :] = jnp.zeros((nhb, c1 * cl, HEAD_DIM), F32)

    def split2(x):
        hi = x.astype(BF16)
        return hi, (x - hi.astype(F32)).astype(BF16)

    def blockdiag(xb):
        return jnp.concatenate([jnp.where(first, xb, zero_b), jnp.where(first, zero_b, xb)], axis=0)

    def pair_mul(x, bd):
        xh, xl = split2(x)
        top = _dot(jnp.concatenate([xh, xl], axis=0), bd[0])
        return top[0:cl] + top[cl:2 * cl] + _dot(xh, bd[1])

    def rhs_pair(a0, a1, b0, b1):
        return jnp.concatenate([jnp.concatenate([a0, a1, zero_h, zero_h], axis=1),
                                jnp.concatenate([zero_h, zero_h, b0, b1], axis=1)], axis=0)

    def pass1(hh, i):
        c = c1 + i * unroll
        cs = range(unroll)
        ps = range(unroll // 2)
        hs = slice(hh * HEAD_DIM, (hh + 1) * HEAD_DIM)
        rows = [pl.ds(pl.multiple_of((c + r) * cl, cl), cl) for r in cs]
        q = [q_ref[rw, hs] for rw in rows]
        k = [k_ref[rw, hs] for rw in rows]
        v = [v_ref[rw, hs] for rw in rows]
        sm = [sm_ref[rw, :] for rw in rows]
        beta = [jnp.sum(jnp.where(lane == h0 + hh, x, 0.0), axis=1, keepdims=True) for x in sm]
        g = [jnp.sum(jnp.where(lane == nh + h0 + hh, x, 0.0), axis=1, keepdims=True) for x in sm]
        sg = [_dot(ll3, jnp.concatenate(_split3(g[r] * wx[r % 2]), axis=0)) for r in cs]
        eseg = [jnp.exp(sg[2 * p][0:cl, 0:2 * cl] + sg[2 * p + 1][0:cl, 0:2 * cl]) for p in ps]
        gcb = [x[0:cl, 2 * cl:4 * cl] for x in sg]
        glb = [x[cl:2 * cl, 2 * cl:4 * cl] for x in sg]
        egc = [jnp.exp(x) for x in gcb]
        kb = [k[r] * beta[r] for r in cs]
        kq = [_dot_nt(jnp.concatenate([kb[2 * p], kb[2 * p + 1], q[2 * p], q[2 * p + 1]], axis=0),
                      jnp.concatenate([k[2 * p], k[2 * p + 1]], axis=0)) for p in ps]
        kk = [jnp.where(first, x[0:cl], x[cl:2 * cl]) for x in kq]
        aqk = [(jnp.where(first, kq[p][2 * cl:3 * cl], kq[p][3 * cl:4 * cl])
                * jnp.where(li >= si, eseg[p], 0.0)).astype(BF16) for p in ps]
        pw = [-(kk[p] * jnp.where(li > si, eseg[p], 0.0)) for p in ps]
        tw = [eye + x for x in pw]
        for _ in range(5):
            bd = [tuple(blockdiag(y) for y in split2(x)) for x in pw]
            pw = [pair_mul(pw[p], bd[p]) for p in ps]
            bd = [tuple(blockdiag(y) for y in split2(x)) for x in pw]
            tw = [tw[p] + pair_mul(tw[p], bd[p]) for p in ps]
        vb = [(v[r] * beta[r]).astype(BF16) for r in cs]
        kbe = [(kb[r] * egc[r]).astype(BF16) for r in cs]
        uw = [_dot(tw[p].astype(BF16), rhs_pair(kbe[2 * p], vb[2 * p], kbe[2 * p + 1], vb[2 * p + 1])) for p in ps]
        wu = [uw[r // 2][:, (r % 2) * 2 * HEAD_DIM:(r % 2 + 1) * 2 * HEAD_DIM] for r in cs]
        wub = [x.astype(BF16) for x in wu]
        awu = [_dot(aqk[p], rhs_pair(wub[2 * p][:, 0:HEAD_DIM], wub[2 * p][:, HEAD_DIM:],
                                     wub[2 * p + 1][:, 0:HEAD_DIM], wub[2 * p + 1][:, HEAD_DIM:])) for p in ps]
        ke = [k[r] * jnp.exp(glb[r] - gcb[r]) for r in cs]
        ab = [_dot_tn(ke[r].astype(BF16), wub[r]) for r in cs]
        aw = [awu[r // 2][:, (r % 2) * 2 * HEAD_DIM:(r % 2) * 2 * HEAD_DIM + HEAD_DIM] for r in cs]
        au = [awu[r // 2][:, (r % 2) * 2 * HEAD_DIM + HEAD_DIM:(r % 2 + 1) * 2 * HEAD_DIM] for r in cs]
        outs = ([x[:, 0:HEAD_DIM].astype(a_s.dtype) for x in ab], [x[:, HEAD_DIM:] for x in ab],
                [(q[r] * egc[r] - aw[r]).astype(qp_s.dtype) for r in cs], au,
                [jnp.exp(x[0:8, :]) for x in glb])
        for ref, nr, vals in zip((a_s, b_s, qp_s, ob_s, eg_s), (HEAD_DIM, HEAD_DIM, cl, cl, 8), outs):
            ref[hh, pl.ds(pl.multiple_of(c * nr, nr), unroll * nr), :] = jnp.concatenate(vals, axis=0)

    for hh in range(nhb):
        lax.fori_loop(0, (nc - c1) // unroll, lambda i, carry, hh=hh: (pass1(hh, i), carry)[1], 0)

    def pass2(c, states):
        heads = range(nhb)
        r0 = pl.ds(pl.multiple_of(c * cl, cl), cl)
        rk = pl.ds(pl.multiple_of(c * HEAD_DIM, HEAD_DIM), HEAD_DIM)
        r8 = pl.ds(pl.multiple_of(c * 8, 8), 8)
        lhs = [jnp.concatenate([a_s[hh, rk, :], qp_s[hh, r0, :]], axis=0) for hh in heads]
        mm = [_dot(lhs[hh], states[hh].astype(BF16)) for hh in heads]
        for hh in heads:
            ob_s[hh, r0, :] = ob_s[hh, r0, :] + mm[hh][HEAD_DIM:, :]
        return tuple(states[hh] * eg_s[hh, r8, :][0:1, :] - mm[hh][0:HEAD_DIM, :] + b_s[hh, rk, :] for hh in heads)

    lax.fori_loop(c0, nc, pass2, tuple(jnp.zeros((HEAD_DIM, HEAD_DIM), F32) for _ in range(nhb)))

    for hh in range(nhb):
        hs = slice(hh * HEAD_DIM, (hh + 1) * HEAD_DIM)
        o = ob_s[hh]
        on = o * lax.rsqrt(jnp.mean(o * o, axis=1, keepdims=True) + RMS_EPS) * ng_ref[...]
        o_ref[:, hs] = (on * _silu(z_ref[:, hs])).astype(o_ref.dtype)


def _gdn(gconv, small, plain, ng, bsz, lp, pad, nh, off_z):
    cl = CHUNK
    nc = lp // cl
    n = nc - pad // cl
    unroll = next(u for u in range(min(GDN_UNROLL, nc), 0, -2) if -(-n // u) * u <= nc)
    assert unroll % 2 == 0
    l = lax.broadcasted_iota(jnp.int32, (2 * cl, 3 * cl), 0)
    jj = lax.broadcasted_iota(jnp.int32, (2 * cl, 3 * cl), 1) % cl
    ll3 = ((l >= cl) | (jj <= l)).astype(BF16)
    j2 = lax.broadcasted_iota(jnp.int32, (2 * cl, 4 * cl), 0)
    s2 = lax.broadcasted_iota(jnp.int32, (2 * cl, 4 * cl), 1)
    tri = (s2 // cl == j2 // cl) & (j2 % cl > s2 % cl)
    wx = jnp.where(s2 < 2 * cl, tri.astype(F32), 1.0)
    nhb = next(m for m in (GDN_HEADS, 1) if nh % m == 0)
    wd = nhb * HEAD_DIM
    bz = off_z // wd
    nb = nh // nhb
    seq = lambda off: pl.BlockSpec((lp, wd), lambda b, h: (b, off + h))
    const = lambda shp: pl.BlockSpec(shp, lambda b, h: (0, 0))
    return pl.pallas_call(
        functools.partial(_gdn_kernel, lp=lp, pad=pad, nh=nh, nhb=nhb, unroll=unroll),
        grid=(bsz, nb),
        in_specs=[seq(0), seq(nb), seq(2 * nb),
                  pl.BlockSpec((lp, LANE), lambda b, h: (b, 0)),
                  seq(bz), const((1, HEAD_DIM)), const((2 * cl, 3 * cl)), const((2 * cl, 4 * cl))],
        out_specs=pl.BlockSpec((lp, wd), lambda b, h: (b, h)),
        out_shape=jax.ShapeDtypeStruct((bsz * lp, nh * HEAD_DIM), BF16),
        scratch_shapes=[pltpu.VMEM((nhb, nc * HEAD_DIM, HEAD_DIM), BF16),
                        pltpu.VMEM((nhb, nc * HEAD_DIM, HEAD_DIM), F32),
                        pltpu.VMEM((nhb, lp, HEAD_DIM), BF16), pltpu.VMEM((nhb, lp, HEAD_DIM), F32),
                        pltpu.VMEM((nhb, nc * 8, HEAD_DIM), F32)],
        compiler_params=_cparams("parallel", "parallel"),
        name="gated_deltanet",
    )(gconv, gconv, gconv, small, plain, ng.reshape(1, HEAD_DIM), ll3, wx)


def _ssd_kernel(x_ref, b_ref, c_ref, sm_ref, z_ref, esel_ref, alog_ref, d_ref, ng_ref, ll_ref, wt_ref,
                incl_ref, o_ref, s_ref, *, rows, hg):
    cl = CHUNK
    wd = hg * SSM_P
    groups = range(SSM_GROUPS)

    @pl.when(pl.program_id(1) == 0)
    def _():
        s_ref[...] = jnp.zeros_like(s_ref)

    wu = wd // SSD_SPLIT
    ll = ll_ref[...]
    wt = wt_ref[:, 0:wu]
    incl = incl_ref[:, 0:wu]
    lane = lax.broadcasted_iota(jnp.int32, (cl, LANE), 1)
    zero_b = jnp.zeros((cl, LANE), BF16)

    def split2(x):
        hi = x.astype(BF16)
        return hi, (x - hi.astype(F32)).astype(BF16)

    def walk(c, units):
        rw = pl.ds(pl.multiple_of(c * cl, cl), cl)
        us = range(len(units))
        gof = [g for g, _ in units]
        ls = [slice(hf * wu, (hf + 1) * wu) for _, hf in units]
        gl = [slice(g * wd + hf * wu, g * wd + (hf + 1) * wu) for g, hf in units]
        x = [x_ref[rw, gl[u]] for u in us]
        bm = [b_ref[rw, g * SSM_STATE:(g + 1) * SSM_STATE] for g in groups]
        cm = [c_ref[rw, g * SSM_STATE:(g + 1) * SSM_STATE] for g in groups]
        sm2 = jnp.concatenate(split2(sm_ref[rw, :]), axis=1)
        dt = [_dot(sm2, esel_ref[gof[u], :, ls[u]]) for u in us]
        la = [dt[u] * -jnp.exp(alog_ref[gof[u], :, ls[u]]) for u in us]
        xs = [x[u] * dt[u] for u in us]
        lab = [split2(la[u]) for u in us]
        sg = [_dot(ll, jnp.concatenate([jnp.concatenate([t * wt, t], axis=1) for t in lab[u]], axis=0))
              for u in us]
        decay = [jnp.where(incl > 0.0, jnp.exp(sg[u][0:cl, 0:wu]), 0.0) for u in us]
        cs = [sg[u][0:cl, wu:2 * wu] for u in us]
        tot = [sg[u][cl:2 * cl, wu:2 * wu] for u in us]
        btile = [jnp.concatenate([bm[g]] * (wu // cl), axis=0) for g in groups]
        scores = [_dot_nt(cm[gof[u]], btile[gof[u]]) for u in us]
        mm = [(scores[u] * decay[u]).astype(BF16) for u in us]
        xb = [xs[u].astype(BF16) for u in us]
        y_diag = []
        for u in us:
            ys = []
            for pp in range(wu // LANE):
                xp = xb[u][:, pp * LANE:(pp + 1) * LANE]
                bd = jnp.concatenate([jnp.where(lane < SSM_P, xp, zero_b), jnp.where(lane < SSM_P, zero_b, xp)],
                                     axis=0)
                ys.append(_dot(mm[u][:, pp * LANE:(pp + 1) * LANE], bd))
            y_diag.append(jnp.concatenate(ys, axis=1))
        state = [s_ref[gof[u], :, ls[u]] for u in us]
        y_off = [_dot(cm[gof[u]], state[u]) * jnp.exp(cs[u]) for u in us]
        ds = [_dot_tn(bm[gof[u]], xs[u] * jnp.exp(tot[u] - cs[u])) for u in us]
        for u in us:
            s_ref[gof[u], :, ls[u]] = state[u] * jnp.exp(tot[u][0:1, :]) + ds[u]
        y = [(y_diag[u] + y_off[u] + d_ref[gof[u], :, ls[u]] * x[u]) * _silu(z_ref[rw, gl[u]]) for u in us]
        sq = [jnp.sum(y[u] * y[u], axis=1, keepdims=True) for u in us]
        for g in sorted(set(gof)):
            rs = lax.rsqrt(sum(sq[u] for u in us if gof[u] == g) * (1.0 / wd) + RMS_EPS)
            for u in us:
                if gof[u] == g:
                    o_ref[rw, gl[u]] = (y[u] * rs * ng_ref[g, :, ls[u]]).astype(o_ref.dtype)

    def body(c, carry):
        walk(c, [(g, hf) for g in groups for hf in range(SSD_SPLIT)])
        return carry

    lax.fori_loop(0, rows // cl, body, 0)


def _ssd(sconv, small, plain, alog, dvec, ng, bsz, lp, n_gdn_heads, inner, off_z):
    cl = CHUNK
    ng_ = SSM_GROUPS
    wd = inner // ng_
    hg = wd // SSM_P
    ns = 4
    rows = lp // ns
    assert rows % cl == 0 and off_z % inner == 0 and inner % (ng_ * SSM_STATE) == 0
    col = lax.broadcasted_iota(jnp.int32, (ng_, 2 * LANE, wd), 1) % LANE
    hh = lax.broadcasted_iota(jnp.int32, (ng_, 2 * LANE, wd), 2) // SSM_P
    gi = lax.broadcasted_iota(jnp.int32, (ng_, 2 * LANE, wd), 0)
    esel = (col == 2 * n_gdn_heads + gi * hg + hh).astype(BF16)
    l = lax.broadcasted_iota(jnp.int32, (2 * cl, 2 * cl), 0)
    jj = lax.broadcasted_iota(jnp.int32, (2 * cl, 2 * cl), 1) % cl
    ll = ((l >= cl) | (jj <= l)).astype(BF16)
    j2 = lax.broadcasted_iota(jnp.int32, (cl, wd), 0)
    s2 = lax.broadcasted_iota(jnp.int32, (cl, wd), 1) % cl
    wt = (j2 > s2).astype(BF16)
    incl = (s2 <= j2).astype(F32)
    expand = lambda t: jnp.repeat(t.reshape(ng_, 1, hg), SSM_P, axis=2)
    nbc = ng_ * SSM_STATE
    rowblk = lambda w, cb: pl.BlockSpec((rows, w), lambda b, s: (b * ns + s, cb))
    full = lambda shp: pl.BlockSpec(shp, lambda b, s: (0,) * len(shp))
    return pl.pallas_call(
        functools.partial(_ssd_kernel, rows=rows, hg=hg),
        grid=(bsz, ns),
        in_specs=[rowblk(inner, 0), rowblk(nbc, inner // nbc), rowblk(nbc, inner // nbc + 1), rowblk(LANE, 0),
                  rowblk(inner, off_z // inner),
                  full((ng_, 2 * LANE, wd)), full((ng_, 1, wd)), full((ng_, 1, wd)), full((ng_, 1, wd)),
                  full((2 * cl, 2 * cl)), full((cl, wd)), full((cl, wd))],
        out_specs=rowblk(inner, 0),
        out_shape=jax.ShapeDtypeStruct((bsz * lp, inner), BF16),
        scratch_shapes=[pltpu.VMEM((ng_, SSM_STATE, wd), F32)],
        compiler_params=_cparams("parallel", "arbitrary"),
        name="ssd",
    )(sconv, sconv, sconv, small, plain, esel, expand(alog), expand(dvec), ng.reshape(ng_, 1, wd),
      ll, wt, incl)


def _out_kernel(oa_ref, ob_ref, oc_ref, ga_ref, gb_ref, gc_ref, h_ref, wa_ref, wb_ref, wc_ref, wo_ref,
                g_ref, hn_ref, xn_ref):
    merged = (_sigmoid(ga_ref[...]) * _dot(oa_ref[...], wa_ref[...])
              + _sigmoid(gb_ref[...]) * _dot(ob_ref[...], wb_ref[...])
              + _sigmoid(gc_ref[...]) * _dot(oc_ref[...], wc_ref[...]))
    hn = h_ref[...] + _dot(merged.astype(BF16), wo_ref[...])
    hn_ref[...] = hn
    y = hn * lax.rsqrt(jnp.mean(hn * hn, axis=-1, keepdims=True) + RMS_EPS) * g_ref[...]
    xn_ref[...] = y.astype(xn_ref.dtype)


def _out_stage(oa, ob, oc, plain, h, wa, wb, wc, wo, g_next, off_gates, xn_dtype):
    tp, d = h.shape
    tm = _pick(tp, (256, 128))
    bg = off_gates // d
    row = lambda w, cb=0: pl.BlockSpec((tm, w), lambda i: (i, cb))
    full = lambda a: pl.BlockSpec(a.shape, lambda i: (0, 0))
    return pl.pallas_call(
        _out_kernel,
        grid=(tp // tm,),
        in_specs=[row(oa.shape[1]), row(ob.shape[1]), row(oc.shape[1]),
                  row(d, bg), row(d, bg + 1), row(d, bg + 2), row(d),
                  full(wa), full(wb), full(wc), full(wo), pl.BlockSpec((1, d), lambda i: (0, 0))],
        out_specs=[row(d), row(d)],
        out_shape=[jax.ShapeDtypeStruct((tp, d), F32), jax.ShapeDtypeStruct((tp, d), xn_dtype)],
        compiler_params=_cparams("parallel"),
        name="merge_out",
    )(oa, ob, oc, plain, plain, plain, h, wa, wb, wc, wo, g_next.reshape(1, d))


def kernel(x, meta_tokens, norm_g, w_in, gdn_conv_w, gdn_a_log, gdn_dt_bias, gdn_norm_g, ssm_conv_w,
           ssm_conv_b, ssm_a_log, ssm_dt_bias, ssm_d, ssm_norm_g, w_branch_a, w_branch_b, w_branch_c,
           w_out, final_norm_g):
    bsz, seq, d = x.shape
    depth = norm_g.shape[0]
    sbw = w_branch_a.shape[1]
    vw = w_branch_b.shape[1]
    inner = w_branch_c.shape[1]
    hgdn = gdn_a_log.shape[1]
    hssm = ssm_a_log.shape[1]
    conv_ch = ssm_conv_w.shape[2]
    qkw = (gdn_conv_w.shape[2] - vw) // 2
    hsb = sbw // HEAD_DIM
    assert sbw == d and vw == d and qkw == hgdn * HEAD_DIM and vw == hgdn * HEAD_DIM
    assert inner == hssm * SSM_P and conv_ch == inner + 2 * SSM_GROUPS * SSM_STATE
    assert 2 * hgdn + hssm <= LANE and (inner // SSM_GROUPS) % (2 * SSM_P) == 0

    lreal = N_META + seq
    lp = -(-(lreal + CONV_K - 1) // ATT_BLOCK) * ATT_BLOCK
    pad = lp - lreal
    assert (lp // 2) % CHUNK == 0

    o_sb = 0
    o_gq = 4 * sbw
    o_gz = o_gq + 2 * qkw + vw
    o_gb = o_gz + vw
    o_ga = o_gb + hgdn
    o_sz = o_ga + hgdn
    o_sx = o_sz + inner
    o_dt = o_sx + conv_ch
    o_gt = o_dt + hssm
    assert w_in.shape[2] == o_gt + 3 * d
    w_plain = jnp.concatenate([w_in[:, :, o_sb:o_gq], w_in[:, :, o_gz:o_gb], w_in[:, :, o_gt:o_gt + 3 * d],
                               w_in[:, :, o_sz:o_sx]], axis=2).astype(BF16)
    p_gz = 4 * sbw
    p_gt = p_gz + vw
    p_sz = p_gt + 3 * d
    w_gconv = w_in[:, :, o_gq:o_gz].astype(BF16)
    w_sconv = w_in[:, :, o_sx:o_dt].astype(BF16)
    nsmall = 2 * hgdn + hssm
    w_small = jnp.concatenate([w_in[:, :, o_gb:o_sz], w_in[:, :, o_dt:o_gt],
                               jnp.zeros((depth, d, LANE - nsmall), F32)], axis=2).astype(BF16)
    zpad = lambda t: jnp.concatenate([t, jnp.zeros((depth, LANE - t.shape[1]), F32)], axis=1)
    kind = jnp.concatenate([jnp.zeros((hgdn,), F32), jnp.ones((hgdn,), F32), jnp.full((hssm,), 2.0, F32),
                            jnp.full((LANE - nsmall,), 3.0, F32)])
    small_par = jnp.stack([
        jnp.broadcast_to(kind, (depth, LANE)),
        zpad(jnp.concatenate([jnp.zeros((depth, hgdn), F32), gdn_a_log], axis=1)),
        zpad(jnp.concatenate([jnp.zeros((depth, hgdn), F32), gdn_dt_bias, ssm_dt_bias], axis=1)),
    ] + [jnp.zeros((depth, LANE), F32)] * 5, axis=1)
    wa, wb, wc, wo = (t.astype(BF16) for t in (w_branch_a, w_branch_b, w_branch_c, w_out))
    gconv_b = jnp.zeros((1, w_gconv.shape[2]), F32)

    meta = jnp.broadcast_to(meta_tokens.astype(x.dtype)[None], (bsz, N_META, d))
    h = jnp.concatenate([jnp.zeros((bsz, pad, d), x.dtype), meta, x], axis=1).reshape(bsz * lp, d)
    xn = _rmsnorm(h, norm_g[0], BF16)

    tn_plain = _pick(w_plain.shape[2], (512, 256))
    tn_g = _pick(qkw, (256, 128))
    tn_s = _pick(conv_ch, (256, 128))
    conv_specs = lambda tn: (pl.BlockSpec((CONV_K, tn), lambda b, j: (0, j)),
                             pl.BlockSpec((1, tn), lambda b, j: (0, j)))
    for layer in range(depth):
        plain = _inproj(xn, w_plain[layer], bsz, lp, tn=tn_plain, kernel=_inproj_plain_kernel, name="inproj_plain")
        gconv = _inproj(xn, w_gconv[layer], bsz, lp, tn=tn_g,
                        kernel=functools.partial(_inproj_conv_kernel, pad=pad, norm_blocks=qkw // tn_g,
                                                 q_scale=HEAD_DIM ** -0.5),
                        extra=(gdn_conv_w[layer], gconv_b), extra_specs=conv_specs(tn_g), name="inproj_gdn_conv")
        sconv = _inproj(xn, w_sconv[layer], bsz, lp, tn=tn_s,
                        kernel=functools.partial(_inproj_conv_kernel, pad=pad, norm_blocks=0, q_scale=1.0),
                        extra=(ssm_conv_w[layer], ssm_conv_b[layer].reshape(1, conv_ch)),
                        extra_specs=conv_specs(tn_s), name="inproj_ssm_conv")
        small = _inproj(xn, w_small[layer], bsz, lp, tn=LANE,
                        kernel=functools.partial(_inproj_small_kernel, pad=pad),
                        extra=(small_par[layer],), extra_specs=(pl.BlockSpec((8, LANE), lambda b, j: (0, 0)),),
                        name="inproj_small")
        oa = _attention(plain, bsz, lp, hsb, 0, sbw, 2 * sbw, 3 * sbw)
        ob = _gdn(gconv, small, plain, gdn_norm_g[layer], bsz, lp, pad, hgdn, p_gz)
        oc = _ssd(sconv, small, plain, ssm_a_log[layer], ssm_d[layer], ssm_norm_g[layer], bsz, lp, hgdn,
                  inner, p_sz)
        last = layer == depth - 1
        g_next = final_norm_g if last else norm_g[layer + 1]
        h, xn = _out_stage(oa, ob, oc, plain, h, wa[layer], wb[layer], wc[layer], wo[layer], g_next, p_gt,
                           F32 if last else BF16)
    return xn.reshape(bsz, lp, d)[:, pad + N_META:]
```

```python
import functools

import jax
import jax.numpy as jnp
from jax import lax
from jax.experimental import pallas as pl
from jax.experimental.pallas import tpu as pltpu

F32 = jnp.float32
BF16 = jnp.bfloat16

N_META = 16
RMS_EPS = 1e-6
L2_EPS = 1e-6
CONV_K = 4
HEAD_DIM = 128
SSM_GROUPS = 2
SSM_STATE = 128
SSM_P = 64
CHUNK = 64
LANE = 128
ATT_BLOCK = 256
VMEM_LIMIT = 52 * 1024 * 1024
LOG2E = 1.4426950408889634
EXP2_CAP = 126.0
GDN_UNROLL = 12
SSD_SPLIT = 1
GDN_HEADS = 2
ATT_HEADS = 4
ATT_LOG_TERMS = 1


def _cparams(*sem):
    return pltpu.CompilerParams(dimension_semantics=sem, vmem_limit_bytes=VMEM_LIMIT)


def _sigmoid(x):
    return 1.0 / (1.0 + jnp.exp(-x))


def _silu(x):
    return x * _sigmoid(x)


def _softplus(x):
    return jnp.maximum(x, 0.0) + jnp.log1p(jnp.exp(-jnp.abs(x)))


def _split3(x):
    hi = x.astype(BF16)
    r = x - hi.astype(F32)
    mid = r.astype(BF16)
    lo = (r - mid.astype(F32)).astype(BF16)
    return hi, mid, lo


def _dot(a, b):
    return jnp.dot(a, b, preferred_element_type=F32)


def _dot_nt(a, b):
    return lax.dot_general(a, b, (((1,), (1,)), ((), ())), preferred_element_type=F32)


def _dot_tn(a, b):
    return lax.dot_general(a, b, (((0,), (0,)), ((), ())), preferred_element_type=F32)


def _dot_hp(a, b):
    ah = a.astype(BF16)
    al = (a - ah.astype(F32)).astype(BF16)
    bh = b.astype(BF16)
    bl = (b - bh.astype(F32)).astype(BF16)
    return _dot(ah, bh) + (_dot(al, bh) + _dot(ah, bl))


def _pick(n, cands):
    for c in cands:
        if n % c == 0:
            return c
    raise ValueError(f"no tile for {n} in {cands}")


def _rmsnorm_kernel(h_ref, g_ref, o_ref):
    x = h_ref[...]
    y = x * lax.rsqrt(jnp.mean(x * x, axis=-1, keepdims=True) + RMS_EPS) * g_ref[...]
    o_ref[...] = y.astype(o_ref.dtype)


def _rmsnorm(h, g, out_dtype):
    tp, d = h.shape
    tm = _pick(tp, (512, 384, 256, 128))
    return pl.pallas_call(
        _rmsnorm_kernel,
        grid=(tp // tm,),
        in_specs=[pl.BlockSpec((tm, d), lambda i: (i, 0)), pl.BlockSpec((1, d), lambda i: (0, 0))],
        out_specs=pl.BlockSpec((tm, d), lambda i: (i, 0)),
        out_shape=jax.ShapeDtypeStruct((tp, d), out_dtype),
        compiler_params=_cparams("parallel"),
        name="rmsnorm",
    )(h, g.reshape(1, d))


def _inproj_plain_kernel(x_ref, w_ref, o_ref):
    o_ref[...] = _dot(x_ref[...], w_ref[...]).astype(o_ref.dtype)


def _inproj_conv_kernel(x_ref, w_ref, cw_ref, cb_ref, o_ref, *, pad, norm_blocks, q_scale):
    acc = _dot(x_ref[...], w_ref[...])
    cw = cw_ref[...]
    y = acc * cw[CONV_K - 1:CONV_K, :]
    for k in range(CONV_K - 1):
        y = y + pltpu.roll(acc, CONV_K - 1 - k, axis=0) * cw[k:k + 1, :]
    y = _silu(y + cb_ref[...])
    if norm_blocks:
        j = pl.program_id(1)
        for s in range(y.shape[1] // HEAD_DIM):
            ys = y[:, s * HEAD_DIM:(s + 1) * HEAD_DIM]
            rs = lax.rsqrt(jnp.sum(ys * ys, axis=1, keepdims=True) + L2_EPS)
            fac = jnp.where(j < norm_blocks, rs * q_scale, jnp.where(j < 2 * norm_blocks, rs, 1.0))
            o_ref[:, s * HEAD_DIM:(s + 1) * HEAD_DIM] = (ys * fac).astype(o_ref.dtype)
    else:
        o_ref[...] = y.astype(o_ref.dtype)
    o_ref[0:pad, :] = jnp.zeros((pad, o_ref.shape[1]), o_ref.dtype)


def _inproj_small_kernel(x_ref, w_ref, p_ref, o_ref, *, pad):
    v = _dot(x_ref[...], w_ref[...])
    kind = p_ref[0:1, :]
    alog = p_ref[1:2, :]
    bias = p_ref[2:3, :]
    sp = _softplus(v + bias)
    out = jnp.where(kind == 0.0, _sigmoid(v),
                    jnp.where(kind == 1.0, -jnp.exp(alog) * sp, jnp.where(kind == 2.0, sp, 0.0)))
    row = lax.broadcasted_iota(jnp.int32, out.shape, 0)
    o_ref[...] = jnp.where(row >= pad, out, 0.0)


def _inproj(xn, w, bsz, lp, *, tn, kernel, extra=(), extra_specs=(), out_dtype=F32, name):
    d = xn.shape[1]
    n = w.shape[1]
    return pl.pallas_call(
        kernel,
        grid=(bsz, n // tn),
        in_specs=[pl.BlockSpec((lp, d), lambda b, j: (b, 0)),
                  pl.BlockSpec((d, tn), lambda b, j: (0, j)), *extra_specs],
        out_specs=pl.BlockSpec((lp, tn), lambda b, j: (b, j)),
        out_shape=jax.ShapeDtypeStruct((bsz * lp, n), out_dtype),
        compiler_params=_cparams("parallel", "arbitrary"),
        name=name,
    )(xn, w, *extra)


def _attn_kernel(q_ref, k_ref, v_ref, z_ref, uu_ref, o_ref, *, blk, nhb, scale):
    i = pl.program_id(2)
    uu = uu_ref[...]
    causal = (lax.broadcasted_iota(jnp.int32, (blk, blk), 1) < lax.broadcasted_iota(jnp.int32, (blk, blk), 0))
    qs = [(q_ref[:, hh * HEAD_DIM:(hh + 1) * HEAD_DIM] * (-scale * LOG2E)).astype(BF16) for hh in range(nhb)]

    def logits(hh, j):
        r0 = pl.multiple_of(j * blk, blk)
        kb = k_ref[pl.ds(r0, blk), hh * HEAD_DIM:(hh + 1) * HEAD_DIM].astype(BF16)
        return _dot_nt(qs[hh], kb)

    def log_terms(zn, diag):
        zn = jnp.minimum(zn, EXP2_CAP)
        sp = jnp.log2(1.0 + jnp.exp2(zn))
        lk = zn - sp
        if diag:
            lk = jnp.where(causal, lk, 0.0)
        terms, resid = [], lk
        for t in range(ATT_LOG_TERMS):
            terms.append(resid.astype(BF16))
            if t + 1 < ATT_LOG_TERMS:
                resid = resid - terms[-1].astype(F32)
        return sp, lk, jnp.concatenate(terms, axis=1) if len(terms) > 1 else terms[0]

    def weights(sp, later, diag):
        w = jnp.exp2(later - sp)
        if diag:
            w = jnp.where(causal, w, 0.0)
        return w.astype(BF16)

    def values(hh, j):
        r0 = pl.multiple_of(j * blk, blk)
        return v_ref[pl.ds(r0, blk), hh * HEAD_DIM:(hh + 1) * HEAD_DIM].astype(BF16)

    def tiles(j, acc, run, diag):
        heads = range(nhb)
        zs = [logits(hh, j) for hh in heads]
        lt = [log_terms(z, diag) for z in zs]
        later = [_dot(lt[hh][2], uu) + run[hh * blk:(hh + 1) * blk] for hh in heads]
        ws = [weights(lt[hh][0], later[hh], diag) for hh in heads]
        pv = [_dot(ws[hh], values(hh, j)) for hh in heads]
        rs = [jnp.sum(lt[hh][1], axis=1, keepdims=True) for hh in heads]
        return acc + jnp.concatenate(pv, axis=0), run + jnp.concatenate(rs, axis=0)

    carry = tiles(i, jnp.zeros((nhb * blk, HEAD_DIM), F32), jnp.zeros((nhb * blk, 1), F32), True)
    acc, _ = lax.fori_loop(1, i + 1, lambda step, c: tiles(i - step, *c, False), carry)
    for hh in range(nhb):
        sl = slice(hh * HEAD_DIM, (hh + 1) * HEAD_DIM)
        o_ref[:, sl] = (acc[hh * blk:(hh + 1) * blk] * _silu(z_ref[:, sl])).astype(o_ref.dtype)


def _attention(plain, bsz, lp, nh, off_q, off_k, off_v, off_z):
    blk = ATT_BLOCK
    nq = lp // blk
    nhb = next(n for n in (ATT_HEADS, 2, 1) if nh % n == 0)
    wd = nhb * HEAD_DIM
    j = lax.broadcasted_iota(jnp.int32, (ATT_LOG_TERMS * blk, blk), 0) % blk
    s = lax.broadcasted_iota(jnp.int32, (ATT_LOG_TERMS * blk, blk), 1)
    uu = (j > s).astype(BF16)
    bq, bk, bv, bz = (o // wd for o in (off_q, off_k, off_v, off_z))
    return pl.pallas_call(
        functools.partial(_attn_kernel, blk=blk, nhb=nhb, scale=HEAD_DIM ** -0.5),
        grid=(bsz, nh // nhb, nq),
        in_specs=[pl.BlockSpec((blk, wd), lambda b, h, i: (b * nq + i, bq + h)),
                  pl.BlockSpec((lp, wd), lambda b, h, i: (b, bk + h)),
                  pl.BlockSpec((lp, wd), lambda b, h, i: (b, bv + h)),
                  pl.BlockSpec((blk, wd), lambda b, h, i: (b * nq + i, bz + h)),
                  pl.BlockSpec((ATT_LOG_TERMS * blk, blk), lambda b, h, i: (0, 0))],
        out_specs=pl.BlockSpec((blk, wd), lambda b, h, i: (b * nq + i, h)),
        out_shape=jax.ShapeDtypeStruct((bsz * lp, nh * HEAD_DIM), BF16),
        compiler_params=_cparams("parallel", "parallel", "arbitrary"),
        name="sb_attention",
    )(plain, plain, plain, plain, uu)


def _gdn_kernel(q_ref, k_ref, v_ref, sm_ref, z_ref, ng_ref, ll3_ref, wx_ref, o_ref,
                a_s, b_s, qp_s, ob_s, eg_s, *, lp, pad, nh, nhb, unroll):
    h0 = pl.program_id(1) * nhb
    cl = CHUNK
    nc = lp // cl
    c0 = pad // cl
    c1 = nc - (nc - c0 + unroll - 1) // unroll * unroll
    assert c1 >= 0
    ll3 = ll3_ref[...]
    wx = (wx_ref[0:cl, :], wx_ref[cl:2 * cl, :])
    li = lax.broadcasted_iota(jnp.int32, (cl, 2 * cl), 0)
    lanew = lax.broadcasted_iota(jnp.int32, (cl, 2 * cl), 1)
    first = lanew < cl
    si = lanew & (cl - 1)
    eye = (li == si).astype(F32)
    lane = lax.broadcasted_iota(jnp.int32, (cl, LANE), 1)
    zero_b = jnp.zeros((cl, 2 * cl), BF16)
    zero_h = jnp.zeros((cl, HEAD_DIM), BF16)
    if c1:
        ob_s[:, 0:c1 * cl, :] = jnp.zeros((nhb, c1 * cl, HEAD_DIM), F32)

    def split2(x):
        hi = x.astype(BF16)
        return hi, (x - hi.astype(F32)).astype(BF16)

    def blockdiag(xb):
        return jnp.concatenate([jnp.where(first, xb, zero_b), jnp.where(first, zero_b, xb)], axis=0)

    def pair_mul(x, bd):
        xh, xl = split2(x)
        top = _dot(jnp.concatenate([xh, xl], axis=0), bd[0])
        return top[0:cl] + top[cl:2 * cl] + _dot(xh, bd[1])

    def rhs_pair(a0, a1, b0, b1):
        return jnp.concatenate([jnp.concatenate([a0, a1, zero_h, zero_h], axis=1),
                                jnp.concatenate([zero_h, zero_h, b0, b1], axis=1)], axis=0)

    def pass1(hh, i):
        c = c1 + i * unroll
        cs = range(unroll)
        ps = range(unroll // 2)
        hs = slice(hh * HEAD_DIM, (hh + 1) * HEAD_DIM)
        rows = [pl.ds(pl.multiple_of((c + r) * cl, cl), cl) for r in cs]
        q = [q_ref[rw, hs] for rw in rows]
        k = [k_ref[rw, hs] for rw in rows]
        v = [v_ref[rw, hs] for rw in rows]
        sm = [sm_ref[rw, :] for rw in rows]
        beta = [jnp.sum(jnp.where(lane == h0 + hh, x, 0.0), axis=1, keepdims=True) for x in sm]
        g = [jnp.sum(jnp.where(lane == nh + h0 + hh, x, 0.0), axis=1, keepdims=True) for x in sm]
        sg = [_dot(ll3, jnp.concatenate(_split3(g[r] * wx[r % 2]), axis=0)) for r in cs]
        eseg = [jnp.exp(sg[2 * p][0:cl, 0:2 * cl] + sg[2 * p + 1][0:cl, 0:2 * cl]) for p in ps]
        gcb = [x[0:cl, 2 * cl:4 * cl] for x in sg]
        glb = [x[cl:2 * cl, 2 * cl:4 * cl] for x in sg]
        egc = [jnp.exp(x) for x in gcb]
        kb = [k[r] * beta[r] for r in cs]
        kq = [_dot_nt(jnp.concatenate([kb[2 * p], kb[2 * p + 1], q[2 * p], q[2 * p + 1]], axis=0),
                      jnp.concatenate([k[2 * p], k[2 * p + 1]], axis=0)) for p in ps]
        kk = [jnp.where(first, x[0:cl], x[cl:2 * cl]) for x in kq]
        aqk = [(jnp.where(first, kq[p][2 * cl:3 * cl], kq[p][3 * cl:4 * cl])
                * jnp.where(li >= si, eseg[p], 0.0)).astype(BF16) for p in ps]
        pw = [-(kk[p] * jnp.where(li > si, eseg[p], 0.0)) for p in ps]
        tw = [eye + x for x in pw]
        for _ in range(5):
            bd = [tuple(blockdiag(y) for y in split2(x)) for x in pw]
            pw = [pair_mul(pw[p], bd[p]) for p in ps]
            bd = [tuple(blockdiag(y) for y in split2(x)) for x in pw]
            tw = [tw[p] + pair_mul(tw[p], bd[p]) for p in ps]
        vb = [(v[r] * beta[r]).astype(BF16) for r in cs]
        kbe = [(kb[r] * egc[r]).astype(BF16) for r in cs]
        uw = [_dot(tw[p].astype(BF16), rhs_pair(kbe[2 * p], vb[2 * p], kbe[2 * p + 1], vb[2 * p + 1])) for p in ps]
        wu = [uw[r // 2][:, (r % 2) * 2 * HEAD_DIM:(r % 2 + 1) * 2 * HEAD_DIM] for r in cs]
        wub = [x.astype(BF16) for x in wu]
        awu = [_dot(aqk[p], rhs_pair(wub[2 * p][:, 0:HEAD_DIM], wub[2 * p][:, HEAD_DIM:],
                                     wub[2 * p + 1][:, 0:HEAD_DIM], wub[2 * p + 1][:, HEAD_DIM:])) for p in ps]
        ke = [k[r] * jnp.exp(glb[r] - gcb[r]) for r in cs]
        ab = [_dot_tn(ke[r].astype(BF16), wub[r]) for r in cs]
        aw = [awu[r // 2][:, (r % 2) * 2 * HEAD_DIM:(r % 2) * 2 * HEAD_DIM + HEAD_DIM] for r in cs]
        au = [awu[r // 2][:, (r % 2) * 2 * HEAD_DIM + HEAD_DIM:(r % 2 + 1) * 2 * HEAD_DIM] for r in cs]
        outs = ([x[:, 0:HEAD_DIM].astype(a_s.dtype) for x in ab], [x[:, HEAD_DIM:] for x in ab],
                [(q[r] * egc[r] - aw[r]).astype(qp_s.dtype) for r in cs], au,
                [jnp.exp(x[0:8, :]) for x in glb])
        for ref, nr, vals in zip((a_s, b_s, qp_s, ob_s, eg_s), (HEAD_DIM, HEAD_DIM, cl, cl, 8), outs):
            ref[hh, pl.ds(pl.multiple_of(c * nr, nr), unroll * nr), :] = jnp.concatenate(vals, axis=0)

    for hh in range(nhb):
        lax.fori_loop(0, (nc - c1) // unroll, lambda i, carry, hh=hh: (pass1(hh, i), carry)[1], 0)

    def pass2(c, states):
        heads = range(nhb)
        r0 = pl.ds(pl.multiple_of(c * cl, cl), cl)
        rk = pl.ds(pl.multiple_of(c * HEAD_DIM, HEAD_DIM), HEAD_DIM)
        r8 = pl.ds(pl.multiple_of(c * 8, 8), 8)
        lhs = [jnp.concatenate([a_s[hh, rk, :], qp_s[hh, r0, :]], axis=0) for hh in heads]
        mm = [_dot(lhs[hh], states[hh].astype(BF16)) for hh in heads]
        for hh in heads:
            ob_s[hh, r0, :] = ob_s[hh, r0, :] + mm[hh][HEAD_DIM:, :]
        return tuple(states[hh] * eg_s[hh, r8, :][0:1, :] - mm[hh][0:HEAD_DIM, :] + b_s[hh, rk, :] for hh in heads)

    lax.fori_loop(c0, nc, pass2, tuple(jnp.zeros((HEAD_DIM, HEAD_DIM), F32) for _ in range(nhb)))

    for hh in range(nhb):
        hs = slice(hh * HEAD_DIM, (hh + 1) * HEAD_DIM)
        o = ob_s[hh]
        on = o * lax.rsqrt(jnp.mean(o * o, axis=1, keepdims=True) + RMS_EPS) * ng_ref[...]
        o_ref[:, hs] = (on * _silu(z_ref[:, hs])).astype(o_ref.dtype)


def _gdn(gconv, small, plain, ng, bsz, lp, pad, nh, off_z):
    cl = CHUNK
    nc = lp // cl
    n = nc - pad // cl
    unroll = next(u for u in range(min(GDN_UNROLL, nc), 0, -2) if -(-n // u) * u <= nc)
    assert unroll % 2 == 0
    l = lax.broadcasted_iota(jnp.int32, (2 * cl, 3 * cl), 0)
    jj = lax.broadcasted_iota(jnp.int32, (2 * cl, 3 * cl), 1) % cl
    ll3 = ((l >= cl) | (jj <= l)).astype(BF16)
    j2 = lax.broadcasted_iota(jnp.int32, (2 * cl, 4 * cl), 0)
    s2 = lax.broadcasted_iota(jnp.int32, (2 * cl, 4 * cl), 1)
    tri = (s2 // cl == j2 // cl) & (j2 % cl > s2 % cl)
    wx = jnp.where(s2 < 2 * cl, tri.astype(F32), 1.0)
    nhb = next(m for m in (GDN_HEADS, 1) if nh % m == 0)
    wd = nhb * HEAD_DIM
    bz = off_z // wd
    nb = nh // nhb
    seq = lambda off: pl.BlockSpec((lp, wd), lambda b, h: (b, off + h))
    const = lambda shp: pl.BlockSpec(shp, lambda b, h: (0, 0))
    return pl.pallas_call(
        functools.partial(_gdn_kernel, lp=lp, pad=pad, nh=nh, nhb=nhb, unroll=unroll),
        grid=(bsz, nb),
        in_specs=[seq(0), seq(nb), seq(2 * nb),
                  pl.BlockSpec((lp, LANE), lambda b, h: (b, 0)),
                  seq(bz), const((1, HEAD_DIM)), const((2 * cl, 3 * cl)), const((2 * cl, 4 * cl))],
        out_specs=pl.BlockSpec((lp, wd), lambda b, h: (b, h)),
        out_shape=jax.ShapeDtypeStruct((bsz * lp, nh * HEAD_DIM), BF16),
        scratch_shapes=[pltpu.VMEM((nhb, nc * HEAD_DIM, HEAD_DIM), BF16),
                        pltpu.VMEM((nhb, nc * HEAD_DIM, HEAD_DIM), F32),
                        pltpu.VMEM((nhb, lp, HEAD_DIM), BF16), pltpu.VMEM((nhb, lp, HEAD_DIM), F32),
                        pltpu.VMEM((nhb, nc * 8, HEAD_DIM), F32)],
        compiler_params=_cparams("parallel", "parallel"),
        name="gated_deltanet",
    )(gconv, gconv, gconv, small, plain, ng.reshape(1, HEAD_DIM), ll3, wx)


def _ssd_kernel(x_ref, b_ref, c_ref, sm_ref, z_ref, esel_ref, alog_ref, d_ref, ng_ref, ll_ref, wt_ref,
                incl_ref, o_ref, s_ref, *, rows, hg):
    cl = CHUNK
    wd = hg * SSM_P
    groups = range(SSM_GROUPS)

    @pl.when(pl.program_id(1) == 0)
    def _():
        s_ref[...] = jnp.zeros_like(s_ref)

    wu = wd // SSD_SPLIT
    ll = ll_ref[...]
    wt = wt_ref[:, 0:wu]
    incl = incl_ref[:, 0:wu]
    lane = lax.broadcasted_iota(jnp.int32, (cl, LANE), 1)
    zero_b = jnp.zeros((cl, LANE), BF16)

    def split2(x):
        hi = x.astype(BF16)
        return hi, (x - hi.astype(F32)).astype(BF16)

    def walk(c, units):
        rw = pl.ds(pl.multiple_of(c * cl, cl), cl)
        us = range(len(units))
        gof = [g for g, _ in units]
        ls = [slice(hf * wu, (hf + 1) * wu) for _, hf in units]
        gl = [slice(g * wd + hf * wu, g * wd + (hf + 1) * wu) for g, hf in units]
        x = [x_ref[rw, gl[u]] for u in us]
        bm = [b_ref[rw, g * SSM_STATE:(g + 1) * SSM_STATE] for g in groups]
        cm = [c_ref[rw, g * SSM_STATE:(g + 1) * SSM_STATE] for g in groups]
        sm2 = jnp.concatenate(split2(sm_ref[rw, :]), axis=1)
        dt = [_dot(sm2, esel_ref[gof[u], :, ls[u]]) for u in us]
        la = [dt[u] * -jnp.exp(alog_ref[gof[u], :, ls[u]]) for u in us]
        xs = [x[u] * dt[u] for u in us]
        lab = [split2(la[u]) for u in us]
        sg = [_dot(ll, jnp.concatenate([jnp.concatenate([t * wt, t], axis=1) for t in lab[u]], axis=0))
              for u in us]
        decay = [jnp.where(incl > 0.0, jnp.exp(sg[u][0:cl, 0:wu]), 0.0) for u in us]
        cs = [sg[u][0:cl, wu:2 * wu] for u in us]
        tot = [sg[u][cl:2 * cl, wu:2 * wu] for u in us]
        btile = [jnp.concatenate([bm[g]] * (wu // cl), axis=0) for g in groups]
        scores = [_dot_nt(cm[gof[u]], btile[gof[u]]) for u in us]
        mm = [(scores[u] * decay[u]).astype(BF16) for u in us]
        xb = [xs[u].astype(BF16) for u in us]
        y_diag = []
        for u in us:
            ys = []
            for pp in range(wu // LANE):
                xp = xb[u][:, pp * LANE:(pp + 1) * LANE]
                bd = jnp.concatenate([jnp.where(lane < SSM_P, xp, zero_b), jnp.where(lane < SSM_P, zero_b, xp)],
                                     axis=0)
                ys.append(_dot(mm[u][:, pp * LANE:(pp + 1) * LANE], bd))
            y_diag.append(jnp.concatenate(ys, axis=1))
        state = [s_ref[gof[u], :, ls[u]] for u in us]
        y_off = [_dot(cm[gof[u]], state[u]) * jnp.exp(cs[u]) for u in us]
        ds = [_dot_tn(bm[gof[u]], xs[u] * jnp.exp(tot[u] - cs[u])) for u in us]
        for u in us:
            s_ref[gof[u], :, ls[u]] = state[u] * jnp.exp(tot[u][0:1, :]) + ds[u]
        y = [(y_diag[u] + y_off[u] + d_ref[gof[u], :, ls[u]] * x[u]) * _silu(z_ref[rw, gl[u]]) for u in us]
        sq = [jnp.sum(y[u] * y[u], axis=1, keepdims=True) for u in us]
        for g in sorted(set(gof)):
            rs = lax.rsqrt(sum(sq[u] for u in us if gof[u] == g) * (1.0 / wd) + RMS_EPS)
            for u in us:
                if gof[u] == g:
                    o_ref[rw, gl[u]] = (y[u] * rs * ng_ref[g, :, ls[u]]).astype(o_ref.dtype)

    def body(c, carry):
        walk(c, [(g, hf) for g in groups for hf in range(SSD_SPLIT)])
        return carry

    lax.fori_loop(0, rows // cl, body, 0)


def _ssd(sconv, small, plain, alog, dvec, ng, bsz, lp, n_gdn_heads, inner, off_z):
    cl = CHUNK
    ng_ = SSM_GROUPS
    wd = inner // ng_
    hg = wd // SSM_P
    ns = 4
    rows = lp // ns
    assert rows % cl == 0 and off_z % inner == 0 and inner % (ng_ * SSM_STATE) == 0
    col = lax.broadcasted_iota(jnp.int32, (ng_, 2 * LANE, wd), 1) % LANE
    hh = lax.broadcasted_iota(jnp.int32, (ng_, 2 * LANE, wd), 2) // SSM_P
    gi = lax.broadcasted_iota(jnp.int32, (ng_, 2 * LANE, wd), 0)
    esel = (col == 2 * n_gdn_heads + gi * hg + hh).astype(BF16)
    l = lax.broadcasted_iota(jnp.int32, (2 * cl, 2 * cl), 0)
    jj = lax.broadcasted_iota(jnp.int32, (2 * cl, 2 * cl), 1) % cl
    ll = ((l >= cl) | (jj <= l)).astype(BF16)
    j2 = lax.broadcasted_iota(jnp.int32, (cl, wd), 0)
    s2 = lax.broadcasted_iota(jnp.int32, (cl, wd), 1) % cl
    wt = (j2 > s2).astype(BF16)
    incl = (s2 <= j2).astype(F32)
    expand = lambda t: jnp.repeat(t.reshape(ng_, 1, hg), SSM_P, axis=2)
    nbc = ng_ * SSM_STATE
    rowblk = lambda w, cb: pl.BlockSpec((rows, w), lambda b, s: (b * ns + s, cb))
    full = lambda shp: pl.BlockSpec(shp, lambda b, s: (0,) * len(shp))
    return pl.pallas_call(
        functools.partial(_ssd_kernel, rows=rows, hg=hg),
        grid=(bsz, ns),
        in_specs=[rowblk(inner, 0), rowblk(nbc, inner // nbc), rowblk(nbc, inner // nbc + 1), rowblk(LANE, 0),
                  rowblk(inner, off_z // inner),
                  full((ng_, 2 * LANE, wd)), full((ng_, 1, wd)), full((ng_, 1, wd)), full((ng_, 1, wd)),
                  full((2 * cl, 2 * cl)), full((cl, wd)), full((cl, wd))],
        out_specs=rowblk(inner, 0),
        out_shape=jax.ShapeDtypeStruct((bsz * lp, inner), BF16),
        scratch_shapes=[pltpu.VMEM((ng_, SSM_STATE, wd), F32)],
        compiler_params=_cparams("parallel", "arbitrary"),
        name="ssd",
    )(sconv, sconv, sconv, small, plain, esel, expand(alog), expand(dvec), ng.reshape(ng_, 1, wd),
      ll, wt, incl)


def _out_kernel(oa_ref, ob_ref, oc_ref, ga_ref, gb_ref, gc_ref, h_ref, wa_ref, wb_ref, wc_ref, wo_ref,
                g_ref, hn_ref, xn_ref):
    merged = (_sigmoid(ga_ref[...]) * _dot(oa_ref[...], wa_ref[...])
              + _sigmoid(gb_ref[...]) * _dot(ob_ref[...], wb_ref[...])
              + _sigmoid(gc_ref[...]) * _dot(oc_ref[...], wc_ref[...]))
    hn = h_ref[...] + _dot(merged.astype(BF16), wo_ref[...])
    hn_ref[...] = hn
    y = hn * lax.rsqrt(jnp.mean(hn * hn, axis=-1, keepdims=True) + RMS_EPS) * g_ref[...]
    xn_ref[...] = y.astype(xn_ref.dtype)


def _out_stage(oa, ob, oc, plain, h, wa, wb, wc, wo, g_next, off_gates, xn_dtype):
    tp, d = h.shape
    tm = _pick(tp, (256, 128))
    bg = off_gates // d
    row = lambda w, cb=0: pl.BlockSpec((tm, w), lambda i: (i, cb))
    full = lambda a: pl.BlockSpec(a.shape, lambda i: (0, 0))
    return pl.pallas_call(
        _out_kernel,
        grid=(tp // tm,),
        in_specs=[row(oa.shape[1]), row(ob.shape[1]), row(oc.shape[1]),
                  row(d, bg), row(d, bg + 1), row(d, bg + 2), row(d),
                  full(wa), full(wb), full(wc), full(wo), pl.BlockSpec((1, d), lambda i: (0, 0))],
        out_specs=[row(d), row(d)],
        out_shape=[jax.ShapeDtypeStruct((tp, d), F32), jax.ShapeDtypeStruct((tp, d), xn_dtype)],
        compiler_params=_cparams("parallel"),
        name="merge_out",
    )(oa, ob, oc, plain, plain, plain, h, wa, wb, wc, wo, g_next.reshape(1, d))


def kernel(x, meta_tokens, norm_g, w_in, gdn_conv_w, gdn_a_log, gdn_dt_bias, gdn_norm_g, ssm_conv_w,
           ssm_conv_b, ssm_a_log, ssm_dt_bias, ssm_d, ssm_norm_g, w_branch_a, w_branch_b, w_branch_c,
           w_out, final_norm_g):
    bsz, seq, d = x.shape
    depth = norm_g.shape[0]
    sbw = w_branch_a.shape[1]
    vw = w_branch_b.shape[1]
    inner = w_branch_c.shape[1]
    hgdn = gdn_a_log.shape[1]
    hssm = ssm_a_log.shape[1]
    conv_ch = ssm_conv_w.shape[2]
    qkw = (gdn_conv_w.shape[2] - vw) // 2
    hsb = sbw // HEAD_DIM
    assert sbw == d and vw == d and qkw == hgdn * HEAD_DIM and vw == hgdn * HEAD_DIM
    assert inner == hssm * SSM_P and conv_ch == inner + 2 * SSM_GROUPS * SSM_STATE
    assert 2 * hgdn + hssm <= LANE and (inner // SSM_GROUPS) % (2 * SSM_P) == 0

    lreal = N_META + seq
    lp = -(-(lreal + CONV_K - 1) // ATT_BLOCK) * ATT_BLOCK
    pad = lp - lreal
    assert (lp // 2) % CHUNK == 0

    o_sb = 0
    o_gq = 4 * sbw
    o_gz = o_gq + 2 * qkw + vw
    o_gb = o_gz + vw
    o_ga = o_gb + hgdn
    o_sz = o_ga + hgdn
    o_sx = o_sz + inner
    o_dt = o_sx + conv_ch
    o_gt = o_dt + hssm
    assert w_in.shape[2] == o_gt + 3 * d
    w_sb, w_gz, w_gt, w_sz = (w_in[:, :, a:b].astype(BF16)
                              for a, b in ((o_sb, o_gq), (o_gz, o_gb), (o_gt, o_gt + 3 * d), (o_sz, o_sx)))
    w_gconv = w_in[:, :, o_gq:o_gz].astype(BF16)
    w_sconv = w_in[:, :, o_sx:o_dt].astype(BF16)
    nsmall = 2 * hgdn + hssm
    w_small = jnp.concatenate([w_in[:, :, o_gb:o_sz], w_in[:, :, o_dt:o_gt],
                               jnp.zeros((depth, d, LANE - nsmall), F32)], axis=2).astype(BF16)
    zpad = lambda t: jnp.concatenate([t, jnp.zeros((depth, LANE - t.shape[1]), F32)], axis=1)
    kind = jnp.concatenate([jnp.zeros((hgdn,), F32), jnp.ones((hgdn,), F32), jnp.full((hssm,), 2.0, F32),
                            jnp.full((LANE - nsmall,), 3.0, F32)])
    small_par = jnp.stack([
        jnp.broadcast_to(kind, (depth, LANE)),
        zpad(jnp.concatenate([jnp.zeros((depth, hgdn), F32), gdn_a_log], axis=1)),
        zpad(jnp.concatenate([jnp.zeros((depth, hgdn), F32), gdn_dt_bias, ssm_dt_bias], axis=1)),
    ] + [jnp.zeros((depth, LANE), F32)] * 5, axis=1)
    wa, wb, wc, wo = (t.astype(BF16) for t in (w_branch_a, w_branch_b, w_branch_c, w_out))
    gconv_b = jnp.zeros((1, w_gconv.shape[2]), F32)

    meta = jnp.broadcast_to(meta_tokens.astype(x.dtype)[None], (bsz, N_META, d))
    h = jnp.concatenate([jnp.zeros((bsz, pad, d), x.dtype), meta, x], axis=1).reshape(bsz * lp, d)
    xn = _rmsnorm(h, norm_g[0], BF16)

    plain = lambda w, name: _inproj(xn, w, bsz, lp, tn=_pick(w.shape[1], (1024, 512, 256)), kernel=_inproj_plain_kernel,
                                    name=name)
    tn_g = _pick(qkw, (256, 128))
    tn_s = _pick(conv_ch, (256, 128))
    conv_specs = lambda tn: (pl.BlockSpec((CONV_K, tn), lambda b, j: (0, j)),
                             pl.BlockSpec((1, tn), lambda b, j: (0, j)))
    for layer in range(depth):
        sb = plain(w_sb[layer], "inproj_sb")
        gdz = plain(w_gz[layer], "inproj_gdn_z")
        gates = plain(w_gt[layer], "inproj_gates")
        ssz = plain(w_sz[layer], "inproj_ssm_z")
        gconv = _inproj(xn, w_gconv[layer], bsz, lp, tn=tn_g,
                        kernel=functools.partial(_inproj_conv_kernel, pad=pad, norm_blocks=qkw // tn_g,
                                                 q_scale=HEAD_DIM ** -0.5),
                        extra=(gdn_conv_w[layer], gconv_b), extra_specs=conv_specs(tn_g), name="inproj_gdn_conv")
        sconv = _inproj(xn, w_sconv[layer], bsz, lp, tn=tn_s,
                        kernel=functools.partial(_inproj_conv_kernel, pad=pad, norm_blocks=0, q_scale=1.0),
                        extra=(ssm_conv_w[layer], ssm_conv_b[layer].reshape(1, conv_ch)),
                        extra_specs=conv_specs(tn_s), name="inproj_ssm_conv")
        small = _inproj(xn, w_small[layer], bsz, lp, tn=LANE,
                        kernel=functools.partial(_inproj_small_kernel, pad=pad),
                        extra=(small_par[layer],), extra_specs=(pl.BlockSpec((8, LANE), lambda b, j: (0, 0)),),
                        name="inproj_small")
        oa = _attention(sb, bsz, lp, hsb, 0, sbw, 2 * sbw, 3 * sbw)
        ob = _gdn(gconv, small, gdz, gdn_norm_g[layer], bsz, lp, pad, hgdn, 0)
        oc = _ssd(sconv, small, ssz, ssm_a_log[layer], ssm_d[layer], ssm_norm_g[layer], bsz, lp, hgdn, inner, 0)
        last = layer == depth - 1
        g_next = final_norm_g if last else norm_g[layer + 1]
        h, xn = _out_stage(oa, ob, oc, gates, h, wa[layer], wb[layer], wc[layer], wo[layer], g_next, 0,
                           F32 if last else BF16)
    return xn.reshape(bsz, lp, d)[:, pad + N_META:]
```

```python
import functools

import jax
import jax.numpy as jnp
from jax import lax
from jax.experimental import pallas as pl
from jax.experimental.pallas import tpu as pltpu

F32 = jnp.float32
BF16 = jnp.bfloat16

N_META = 16
RMS_EPS = 1e-6
L2_EPS = 1e-6
CONV_K = 4
HEAD_DIM = 128
SSM_GROUPS = 2
SSM_STATE = 128
SSM_P = 64
CHUNK = 64
LANE = 128
ATT_BLOCK = 256
VMEM_LIMIT = 52 * 1024 * 1024
LOG2E = 1.4426950408889634
EXP2_CAP = 126.0
GDN_UNROLL = 12
SSD_SPLIT = 1
GDN_HEADS = 2
ATT_HEADS = 4
ATT_LOG_TERMS = 1


def _cparams(*sem):
    return pltpu.CompilerParams(dimension_semantics=sem, vmem_limit_bytes=VMEM_LIMIT)


def _sigmoid(x):
    return 1.0 / (1.0 + jnp.exp(-x))


def _silu(x):
    return x * _sigmoid(x)


def _softplus(x):
    return jnp.maximum(x, 0.0) + jnp.log1p(jnp.exp(-jnp.abs(x)))


def _split3(x):
    hi = x.astype(BF16)
    r = x - hi.astype(F32)
    mid = r.astype(BF16)
    lo = (r - mid.astype(F32)).astype(BF16)
    return hi, mid, lo


def _dot(a, b):
    return jnp.dot(a, b, preferred_element_type=F32)


def _dot_nt(a, b):
    return lax.dot_general(a, b, (((1,), (1,)), ((), ())), preferred_element_type=F32)


def _dot_tn(a, b):
    return lax.dot_general(a, b, (((0,), (0,)), ((), ())), preferred_element_type=F32)


def _dot_hp(a, b):
    ah = a.astype(BF16)
    al = (a - ah.astype(F32)).astype(BF16)
    bh = b.astype(BF16)
    bl = (b - bh.astype(F32)).astype(BF16)
    return _dot(ah, bh) + (_dot(al, bh) + _dot(ah, bl))


def _pick(n, cands):
    for c in cands:
        if n % c == 0:
            return c
    raise ValueError(f"no tile for {n} in {cands}")


def _rmsnorm_kernel(h_ref, g_ref, o_ref):
    x = h_ref[...]
    y = x * lax.rsqrt(jnp.mean(x * x, axis=-1, keepdims=True) + RMS_EPS) * g_ref[...]
    o_ref[...] = y.astype(o_ref.dtype)


def _rmsnorm(h, g, out_dtype):
    tp, d = h.shape
    tm = _pick(tp, (512, 384, 256, 128))
    return pl.pallas_call(
        _rmsnorm_kernel,
        grid=(tp // tm,),
        in_specs=[pl.BlockSpec((tm, d), lambda i: (i, 0)), pl.BlockSpec((1, d), lambda i: (0, 0))],
        out_specs=pl.BlockSpec((tm, d), lambda i: (i, 0)),
        out_shape=jax.ShapeDtypeStruct((tp, d), out_dtype),
        compiler_params=_cparams("parallel"),
        name="rmsnorm",
    )(h, g.reshape(1, d))


def _inproj_plain_kernel(x_ref, w_ref, o_ref):
    o_ref[...] = _dot(x_ref[...], w_ref[...]).astype(o_ref.dtype)


def _inproj_conv_kernel(x_ref, w_ref, cw_ref, cb_ref, o_ref, *, pad, norm_blocks, q_scale):
    acc = _dot(x_ref[...], w_ref[...])
    cw = cw_ref[...]
    y = acc * cw[CONV_K - 1:CONV_K, :]
    for k in range(CONV_K - 1):
        y = y + pltpu.roll(acc, CONV_K - 1 - k, axis=0) * cw[k:k + 1, :]
    y = _silu(y + cb_ref[...])
    if norm_blocks:
        j = pl.program_id(1)
        for s in range(y.shape[1] // HEAD_DIM):
            ys = y[:, s * HEAD_DIM:(s + 1) * HEAD_DIM]
            rs = lax.rsqrt(jnp.sum(ys * ys, axis=1, keepdims=True) + L2_EPS)
            fac = jnp.where(j < norm_blocks, rs * q_scale, jnp.where(j < 2 * norm_blocks, rs, 1.0))
            o_ref[:, s * HEAD_DIM:(s + 1) * HEAD_DIM] = (ys * fac).astype(o_ref.dtype)
    else:
        o_ref[...] = y.astype(o_ref.dtype)
    o_ref[0:pad, :] = jnp.zeros((pad, o_ref.shape[1]), o_ref.dtype)


def _inproj_small_kernel(x_ref, w_ref, p_ref, o_ref, *, pad):
    v = _dot(x_ref[...], w_ref[...])
    kind = p_ref[0:1, :]
    alog = p_ref[1:2, :]
    bias = p_ref[2:3, :]
    sp = _softplus(v + bias)
    out = jnp.where(kind == 0.0, _sigmoid(v),
                    jnp.where(kind == 1.0, -jnp.exp(alog) * sp, jnp.where(kind == 2.0, sp, 0.0)))
    row = lax.broadcasted_iota(jnp.int32, out.shape, 0)
    o_ref[...] = jnp.where(row >= pad, out, 0.0)


def _inproj(xn, w, bsz, lp, *, tn, kernel, extra=(), extra_specs=(), out_dtype=F32, name):
    d = xn.shape[1]
    n = w.shape[1]
    return pl.pallas_call(
        kernel,
        grid=(bsz, n // tn),
        in_specs=[pl.BlockSpec((lp, d), lambda b, j: (b, 0)),
                  pl.BlockSpec((d, tn), lambda b, j: (0, j)), *extra_specs],
        out_specs=pl.BlockSpec((lp, tn), lambda b, j: (b, j)),
        out_shape=jax.ShapeDtypeStruct((bsz * lp, n), out_dtype),
        compiler_params=_cparams("parallel", "arbitrary"),
        name=name,
    )(xn, w, *extra)


def _attn_kernel(q_ref, k_ref, v_ref, z_ref, uu_ref, o_ref, *, blk, nhb, scale):
    i = pl.program_id(2)
    uu = uu_ref[...]
    causal = (lax.broadcasted_iota(jnp.int32, (blk, blk), 1) < lax.broadcasted_iota(jnp.int32, (blk, blk), 0))
    qs = [(q_ref[:, hh * HEAD_DIM:(hh + 1) * HEAD_DIM] * (-scale * LOG2E)).astype(BF16) for hh in range(nhb)]

    def logits(hh, j):
        r0 = pl.multiple_of(j * blk, blk)
        kb = k_ref[pl.ds(r0, blk), hh * HEAD_DIM:(hh + 1) * HEAD_DIM].astype(BF16)
        return _dot_nt(qs[hh], kb)

    def log_terms(zn, diag):
        zn = jnp.minimum(zn, EXP2_CAP)
        sp = jnp.log2(1.0 + jnp.exp2(zn))
        lk = zn - sp
        if diag:
            lk = jnp.where(causal, lk, 0.0)
        terms, resid = [], lk
        for t in range(ATT_LOG_TERMS):
            terms.append(resid.astype(BF16))
            if t + 1 < ATT_LOG_TERMS:
                resid = resid - terms[-1].astype(F32)
        return sp, lk, jnp.concatenate(terms, axis=1) if len(terms) > 1 else terms[0]

    def weights(sp, later, diag):
        w = jnp.exp2(later - sp)
        if diag:
            w = jnp.where(causal, w, 0.0)
        return w.astype(BF16)

    def values(hh, j):
        r0 = pl.multiple_of(j * blk, blk)
        return v_ref[pl.ds(r0, blk), hh * HEAD_DIM:(hh + 1) * HEAD_DIM].astype(BF16)

    def tiles(j, acc, run, diag):
        heads = range(nhb)
        zs = [logits(hh, j) for hh in heads]
        lt = [log_terms(z, diag) for z in zs]
        later = [_dot(lt[hh][2], uu) + run[hh * blk:(hh + 1) * blk] for hh in heads]
        ws = [weights(lt[hh][0], later[hh], diag) for hh in heads]
        pv = [_dot(ws[hh], values(hh, j)) for hh in heads]
        rs = [jnp.sum(lt[hh][1], axis=1, keepdims=True) for hh in heads]
        return acc + jnp.concatenate(pv, axis=0), run + jnp.concatenate(rs, axis=0)

    carry = tiles(i, jnp.zeros((nhb * blk, HEAD_DIM), F32), jnp.zeros((nhb * blk, 1), F32), True)
    acc, _ = lax.fori_loop(1, i + 1, lambda step, c: tiles(i - step, *c, False), carry)
    for hh in range(nhb):
        sl = slice(hh * HEAD_DIM, (hh + 1) * HEAD_DIM)
        o_ref[:, sl] = (acc[hh * blk:(hh + 1) * blk] * _silu(z_ref[:, sl])).astype(o_ref.dtype)


def _attention(plain, bsz, lp, nh, off_q, off_k, off_v, off_z):
    blk = ATT_BLOCK
    nq = lp // blk
    nhb = next(n for n in (ATT_HEADS, 2, 1) if nh % n == 0)
    wd = nhb * HEAD_DIM
    j = lax.broadcasted_iota(jnp.int32, (ATT_LOG_TERMS * blk, blk), 0) % blk
    s = lax.broadcasted_iota(jnp.int32, (ATT_LOG_TERMS * blk, blk), 1)
    uu = (j > s).astype(BF16)
    bq, bk, bv, bz = (o // wd for o in (off_q, off_k, off_v, off_z))
    return pl.pallas_call(
        functools.partial(_attn_kernel, blk=blk, nhb=nhb, scale=HEAD_DIM ** -0.5),
        grid=(bsz, nh // nhb, nq),
        in_specs=[pl.BlockSpec((blk, wd), lambda b, h, i: (b * nq + i, bq + h)),
                  pl.BlockSpec((lp, wd), lambda b, h, i: (b, bk + h)),
                  pl.BlockSpec((lp, wd), lambda b, h, i: (b, bv + h)),
                  pl.BlockSpec((blk, wd), lambda b, h, i: (b * nq + i, bz + h)),
                  pl.BlockSpec((ATT_LOG_TERMS * blk, blk), lambda b, h, i: (0, 0))],
        out_specs=pl.BlockSpec((blk, wd), lambda b, h, i: (b * nq + i, h)),
        out_shape=jax.ShapeDtypeStruct((bsz * lp, nh * HEAD_DIM), BF16),
        compiler_params=_cparams("parallel", "parallel", "arbitrary"),
        name="sb_attention",
    )(plain, plain, plain, plain, uu)


def _gdn_kernel(q_ref, k_ref, v_ref, sm_ref, z_ref, ng_ref, ll3_ref, wx_ref, o_ref,
                a_s, b_s, qp_s, ob_s, eg_s, *, lp, pad, nh, nhb, unroll):
    h0 = pl.program_id(1) * nhb
    cl = CHUNK
    nc = lp // cl
    c0 = pad // cl
    c1 = nc - (nc - c0 + unroll - 1) // unroll * unroll
    assert c1 >= 0
    ll3 = ll3_ref[...]
    wx = (wx_ref[0:cl, :], wx_ref[cl:2 * cl, :])
    li = lax.broadcasted_iota(jnp.int32, (cl, 2 * cl), 0)
    lanew = lax.broadcasted_iota(jnp.int32, (cl, 2 * cl), 1)
    first = lanew < cl
    si = lanew & (cl - 1)
    eye = (li == si).astype(F32)
    lane = lax.broadcasted_iota(jnp.int32, (cl, LANE), 1)
    zero_b = jnp.zeros((cl, 2 * cl), BF16)
    zero_h = jnp.zeros((cl, HEAD_DIM), BF16)
    if c1:
        ob_s[:, 0:c1 * cl, :] = jnp.zeros((nhb, c1 * cl, HEAD_DIM), F32)

    def split2(x):
        hi = x.astype(BF16)
        return hi, (x - hi.astype(F32)).astype(BF16)

    def blockdiag(xb):
        return jnp.concatenate([jnp.where(first, xb, zero_b), jnp.where(first, zero_b, xb)], axis=0)

    def pair_mul(x, bd):
        xh, xl = split2(x)
        return _dot(jnp.concatenate([xh, xl, xh], axis=1), jnp.concatenate([bd[0], bd[0], bd[1]], axis=0))

    def rhs_pair(a0, a1, b0, b1):
        return jnp.concatenate([jnp.concatenate([a0, a1, zero_h, zero_h], axis=1),
                                jnp.concatenate([zero_h, zero_h, b0, b1], axis=1)], axis=0)

    def pass1(hh, i):
        c = c1 + i * unroll
        cs = range(unroll)
        ps = range(unroll // 2)
        hs = slice(hh * HEAD_DIM, (hh + 1) * HEAD_DIM)
        rows = [pl.ds(pl.multiple_of((c + r) * cl, cl), cl) for r in cs]
        q = [q_ref[rw, hs] for rw in rows]
        k = [k_ref[rw, hs] for rw in rows]
        v = [v_ref[rw, hs] for rw in rows]
        sm = [sm_ref[rw, :] for rw in rows]
        beta = [jnp.sum(jnp.where(lane == h0 + hh, x, 0.0), axis=1, keepdims=True) for x in sm]
        g = [jnp.sum(jnp.where(lane == nh + h0 + hh, x, 0.0), axis=1, keepdims=True) for x in sm]
        sg = [_dot(ll3, jnp.concatenate(_split3(g[r] * wx[r % 2]), axis=0)) for r in cs]
        eseg = [jnp.exp(sg[2 * p][0:cl, 0:2 * cl] + sg[2 * p + 1][0:cl, 0:2 * cl]) for p in ps]
        gcb = [x[0:cl, 2 * cl:4 * cl] for x in sg]
        glb = [x[cl:2 * cl, 2 * cl:4 * cl] for x in sg]
        egc = [jnp.exp(x) for x in gcb]
        kb = [k[r] * beta[r] for r in cs]
        kq = [_dot_nt(jnp.concatenate([kb[2 * p], kb[2 * p + 1], q[2 * p], q[2 * p + 1]], axis=0),
                      jnp.concatenate([k[2 * p], k[2 * p + 1]], axis=0)) for p in ps]
        kk = [jnp.where(first, x[0:cl], x[cl:2 * cl]) for x in kq]
        aqk = [(jnp.where(first, kq[p][2 * cl:3 * cl], kq[p][3 * cl:4 * cl])
                * jnp.where(li >= si, eseg[p], 0.0)).astype(BF16) for p in ps]
        pw = [-(kk[p] * jnp.where(li > si, eseg[p], 0.0)) for p in ps]
        tw = [eye + x for x in pw]
        bd = [tuple(blockdiag(y) for y in split2(x)) for x in pw]
        pw = [pair_mul(pw[p], bd[p]) for p in ps]
        for rnd in range(5):
            bd = [tuple(blockdiag(y) for y in split2(x)) for x in pw]
            if rnd < 4:
                both = [pair_mul(jnp.concatenate([pw[p], tw[p]], axis=0), bd[p]) for p in ps]
                pw = [x[0:cl] for x in both]
                tw = [tw[p] + both[p][cl:2 * cl] for p in ps]
            else:
                tw = [tw[p] + pair_mul(tw[p], bd[p]) for p in ps]
        vb = [(v[r] * beta[r]).astype(BF16) for r in cs]
        kbe = [(kb[r] * egc[r]).astype(BF16) for r in cs]
        uw = [_dot(tw[p].astype(BF16), rhs_pair(kbe[2 * p], vb[2 * p], kbe[2 * p + 1], vb[2 * p + 1])) for p in ps]
        wu = [uw[r // 2][:, (r % 2) * 2 * HEAD_DIM:(r % 2 + 1) * 2 * HEAD_DIM] for r in cs]
        wub = [x.astype(BF16) for x in wu]
        awu = [_dot(aqk[p], rhs_pair(wub[2 * p][:, 0:HEAD_DIM], wub[2 * p][:, HEAD_DIM:],
                                     wub[2 * p + 1][:, 0:HEAD_DIM], wub[2 * p + 1][:, HEAD_DIM:])) for p in ps]
        ke = [k[r] * jnp.exp(glb[r] - gcb[r]) for r in cs]
        ab = [_dot_tn(ke[r].astype(BF16), wub[r]) for r in cs]
        aw = [awu[r // 2][:, (r % 2) * 2 * HEAD_DIM:(r % 2) * 2 * HEAD_DIM + HEAD_DIM] for r in cs]
        au = [awu[r // 2][:, (r % 2) * 2 * HEAD_DIM + HEAD_DIM:(r % 2 + 1) * 2 * HEAD_DIM] for r in cs]
        outs = ([x[:, 0:HEAD_DIM].astype(a_s.dtype) for x in ab], [x[:, HEAD_DIM:] for x in ab],
                [(q[r] * egc[r] - aw[r]).astype(qp_s.dtype) for r in cs], au,
                [jnp.exp(x[0:8, :]) for x in glb])
        for ref, nr, vals in zip((a_s, b_s, qp_s, ob_s, eg_s), (HEAD_DIM, HEAD_DIM, cl, cl, 8), outs):
            ref[hh, pl.ds(pl.multiple_of(c * nr, nr), unroll * nr), :] = jnp.concatenate(vals, axis=0)

    for hh in range(nhb):
        lax.fori_loop(0, (nc - c1) // unroll, lambda i, carry, hh=hh: (pass1(hh, i), carry)[1], 0)

    def pass2(c, states):
        heads = range(nhb)
        r0 = pl.ds(pl.multiple_of(c * cl, cl), cl)
        rk = pl.ds(pl.multiple_of(c * HEAD_DIM, HEAD_DIM), HEAD_DIM)
        r8 = pl.ds(pl.multiple_of(c * 8, 8), 8)
        lhs = [jnp.concatenate([a_s[hh, rk, :], qp_s[hh, r0, :]], axis=0) for hh in heads]
        mm = [_dot(lhs[hh], states[hh].astype(BF16)) for hh in heads]
        for hh in heads:
            ob_s[hh, r0, :] = ob_s[hh, r0, :] + mm[hh][HEAD_DIM:, :]
        return tuple(states[hh] * eg_s[hh, r8, :][0:1, :] - mm[hh][0:HEAD_DIM, :] + b_s[hh, rk, :] for hh in heads)

    lax.fori_loop(c0, nc, pass2, tuple(jnp.zeros((HEAD_DIM, HEAD_DIM), F32) for _ in range(nhb)))

    for hh in range(nhb):
        hs = slice(hh * HEAD_DIM, (hh + 1) * HEAD_DIM)
        o = ob_s[hh]
        on = o * lax.rsqrt(jnp.mean(o * o, axis=1, keepdims=True) + RMS_EPS) * ng_ref[...]
        o_ref[:, hs] = (on * _silu(z_ref[:, hs])).astype(o_ref.dtype)


def _gdn(gconv, small, plain, ng, bsz, lp, pad, nh, off_z):
    cl = CHUNK
    nc = lp // cl
    n = nc - pad // cl
    unroll = next(u for u in range(min(GDN_UNROLL, nc), 0, -2) if -(-n // u) * u <= nc)
    assert unroll % 2 == 0
    l = lax.broadcasted_iota(jnp.int32, (2 * cl, 3 * cl), 0)
    jj = lax.broadcasted_iota(jnp.int32, (2 * cl, 3 * cl), 1) % cl
    ll3 = ((l >= cl) | (jj <= l)).astype(BF16)
    j2 = lax.broadcasted_iota(jnp.int32, (2 * cl, 4 * cl), 0)
    s2 = lax.broadcasted_iota(jnp.int32, (2 * cl, 4 * cl), 1)
    tri = (s2 // cl == j2 // cl) & (j2 % cl > s2 % cl)
    wx = jnp.where(s2 < 2 * cl, tri.astype(F32), 1.0)
    nhb = next(m for m in (GDN_HEADS, 1) if nh % m == 0)
    wd = nhb * HEAD_DIM
    bz = off_z // wd
    nb = nh // nhb
    seq = lambda off: pl.BlockSpec((lp, wd), lambda b, h: (b, off + h))
    const = lambda shp: pl.BlockSpec(shp, lambda b, h: (0, 0))
    return pl.pallas_call(
        functools.partial(_gdn_kernel, lp=lp, pad=pad, nh=nh, nhb=nhb, unroll=unroll),
        grid=(bsz, nb),
        in_specs=[seq(0), seq(nb), seq(2 * nb),
                  pl.BlockSpec((lp, LANE), lambda b, h: (b, 0)),
                  seq(bz), const((1, HEAD_DIM)), const((2 * cl, 3 * cl)), const((2 * cl, 4 * cl))],
        out_specs=pl.BlockSpec((lp, wd), lambda b, h: (b, h)),
        out_shape=jax.ShapeDtypeStruct((bsz * lp, nh * HEAD_DIM), BF16),
        scratch_shapes=[pltpu.VMEM((nhb, nc * HEAD_DIM, HEAD_DIM), BF16),
                        pltpu.VMEM((nhb, nc * HEAD_DIM, HEAD_DIM), F32),
                        pltpu.VMEM((nhb, lp, HEAD_DIM), BF16), pltpu.VMEM((nhb, lp, HEAD_DIM), F32),
                        pltpu.VMEM((nhb, nc * 8, HEAD_DIM), F32)],
        compiler_params=_cparams("parallel", "parallel"),
        name="gated_deltanet",
    )(gconv, gconv, gconv, small, plain, ng.reshape(1, HEAD_DIM), ll3, wx)


def _ssd_kernel(x_ref, b_ref, c_ref, sm_ref, z_ref, esel_ref, alog_ref, d_ref, ng_ref, ll_ref, wt_ref,
                incl_ref, o_ref, s_ref, *, rows, hg, skip):
    cl = CHUNK
    wd = hg * SSM_P
    groups = range(SSM_GROUPS)

    @pl.when(pl.program_id(1) == 0)
    def _():
        s_ref[...] = jnp.zeros_like(s_ref)

    wu = wd // SSD_SPLIT
    ll = ll_ref[...]
    wt = wt_ref[:, 0:wu]
    incl = incl_ref[:, 0:wu]
    lane = lax.broadcasted_iota(jnp.int32, (cl, LANE), 1)
    zero_b = jnp.zeros((cl, LANE), BF16)

    def split2(x):
        hi = x.astype(BF16)
        return hi, (x - hi.astype(F32)).astype(BF16)

    def walk(c, units):
        rw = pl.ds(pl.multiple_of(c * cl, cl), cl)
        us = range(len(units))
        gof = [g for g, _ in units]
        ls = [slice(hf * wu, (hf + 1) * wu) for _, hf in units]
        gl = [slice(g * wd + hf * wu, g * wd + (hf + 1) * wu) for g, hf in units]
        x = [x_ref[rw, gl[u]] for u in us]
        bm = [b_ref[rw, g * SSM_STATE:(g + 1) * SSM_STATE] for g in groups]
        cm = [c_ref[rw, g * SSM_STATE:(g + 1) * SSM_STATE] for g in groups]
        sm2 = jnp.concatenate(split2(sm_ref[rw, :]), axis=1)
        dt = [_dot(sm2, esel_ref[gof[u], :, ls[u]]) for u in us]
        la = [dt[u] * -jnp.exp(alog_ref[gof[u], :, ls[u]]) for u in us]
        xs = [x[u] * dt[u] for u in us]
        lab = [split2(la[u]) for u in us]
        sg = [_dot(ll, jnp.concatenate([jnp.concatenate([t * wt, t], axis=1) for t in lab[u]], axis=0))
              for u in us]
        decay = [jnp.where(incl > 0.0, jnp.exp(sg[u][0:cl, 0:wu]), 0.0) for u in us]
        cs = [sg[u][0:cl, wu:2 * wu] for u in us]
        tot = [sg[u][cl:2 * cl, wu:2 * wu] for u in us]
        btile = [jnp.concatenate([bm[g]] * (wu // cl), axis=0) for g in groups]
        scores = [_dot_nt(cm[gof[u]], btile[gof[u]]) for u in us]
        mm = [(scores[u] * decay[u]).astype(BF16) for u in us]
        xb = [xs[u].astype(BF16) for u in us]
        y_diag = []
        for u in us:
            ys = []
            for pp in range(wu // LANE):
                xp = xb[u][:, pp * LANE:(pp + 1) * LANE]
                bd = jnp.concatenate([jnp.where(lane < SSM_P, xp, zero_b), jnp.where(lane < SSM_P, zero_b, xp)],
                                     axis=0)
                ys.append(_dot(mm[u][:, pp * LANE:(pp + 1) * LANE], bd))
            y_diag.append(jnp.concatenate(ys, axis=1))
        state = [s_ref[gof[u], :, ls[u]] for u in us]
        y_off = [_dot(cm[gof[u]], state[u]) * jnp.exp(cs[u]) for u in us]
        ds = [_dot_tn(bm[gof[u]], xs[u] * jnp.exp(tot[u] - cs[u])) for u in us]
        for u in us:
            s_ref[gof[u], :, ls[u]] = state[u] * jnp.exp(tot[u][0:1, :]) + ds[u]
        y = [(y_diag[u] + y_off[u] + d_ref[gof[u], :, ls[u]] * x[u]) * _silu(z_ref[rw, gl[u]]) for u in us]
        sq = [jnp.sum(y[u] * y[u], axis=1, keepdims=True) for u in us]
        for g in sorted(set(gof)):
            rs = lax.rsqrt(sum(sq[u] for u in us if gof[u] == g) * (1.0 / wd) + RMS_EPS)
            for u in us:
                if gof[u] == g:
                    o_ref[rw, gl[u]] = (y[u] * rs * ng_ref[g, :, ls[u]]).astype(o_ref.dtype)

    def body(c, carry):
        walk(c, [(g, hf) for g in groups for hf in range(SSD_SPLIT)])
        return carry

    first = pl.program_id(1) == 0
    if skip:
        @pl.when(first)
        def _():
            o_ref[0:skip * cl, :] = jnp.zeros((skip * cl, o_ref.shape[1]), o_ref.dtype)

    lax.fori_loop(jnp.where(first, skip, 0), rows // cl, body, 0)


def _ssd(sconv, small, plain, alog, dvec, ng, bsz, lp, pad, n_gdn_heads, inner, off_z):
    cl = CHUNK
    ng_ = SSM_GROUPS
    wd = inner // ng_
    hg = wd // SSM_P
    ns = 4
    rows = lp // ns
    assert rows % cl == 0 and off_z % inner == 0 and inner % (ng_ * SSM_STATE) == 0
    col = lax.broadcasted_iota(jnp.int32, (ng_, 2 * LANE, wd), 1) % LANE
    hh = lax.broadcasted_iota(jnp.int32, (ng_, 2 * LANE, wd), 2) // SSM_P
    gi = lax.broadcasted_iota(jnp.int32, (ng_, 2 * LANE, wd), 0)
    esel = (col == 2 * n_gdn_heads + gi * hg + hh).astype(BF16)
    l = lax.broadcasted_iota(jnp.int32, (2 * cl, 2 * cl), 0)
    jj = lax.broadcasted_iota(jnp.int32, (2 * cl, 2 * cl), 1) % cl
    ll = ((l >= cl) | (jj <= l)).astype(BF16)
    j2 = lax.broadcasted_iota(jnp.int32, (cl, wd), 0)
    s2 = lax.broadcasted_iota(jnp.int32, (cl, wd), 1) % cl
    wt = (j2 > s2).astype(BF16)
    incl = (s2 <= j2).astype(F32)
    expand = lambda t: jnp.repeat(t.reshape(ng_, 1, hg), SSM_P, axis=2)
    nbc = ng_ * SSM_STATE
    rowblk = lambda w, cb: pl.BlockSpec((rows, w), lambda b, s: (b * ns + s, cb))
    full = lambda shp: pl.BlockSpec(shp, lambda b, s: (0,) * len(shp))
    return pl.pallas_call(
        functools.partial(_ssd_kernel, rows=rows, hg=hg, skip=min(pad // cl, rows // cl)),
        grid=(bsz, ns),
        in_specs=[rowblk(inner, 0), rowblk(nbc, inner // nbc), rowblk(nbc, inner // nbc + 1), rowblk(LANE, 0),
                  rowblk(inner, off_z // inner),
                  full((ng_, 2 * LANE, wd)), full((ng_, 1, wd)), full((ng_, 1, wd)), full((ng_, 1, wd)),
                  full((2 * cl, 2 * cl)), full((cl, wd)), full((cl, wd))],
        out_specs=rowblk(inner, 0),
        out_shape=jax.ShapeDtypeStruct((bsz * lp, inner), BF16),
        scratch_shapes=[pltpu.VMEM((ng_, SSM_STATE, wd), F32)],
        compiler_params=_cparams("parallel", "arbitrary"),
        name="ssd",
    )(sconv, sconv, sconv, small, plain, esel, expand(alog), expand(dvec), ng.reshape(ng_, 1, wd),
      ll, wt, incl)


def _out_kernel(oa_ref, ob_ref, oc_ref, ga_ref, gb_ref, gc_ref, h_ref, wa_ref, wb_ref, wc_ref, wo_ref,
                g_ref, hn_ref, xn_ref):
    merged = (_sigmoid(ga_ref[...]) * _dot(oa_ref[...], wa_ref[...])
              + _sigmoid(gb_ref[...]) * _dot(ob_ref[...], wb_ref[...])
              + _sigmoid(gc_ref[...]) * _dot(oc_ref[...], wc_ref[...]))
    hn = h_ref[...] + _dot(merged.astype(BF16), wo_ref[...])
    hn_ref[...] = hn
    y = hn * lax.rsqrt(jnp.mean(hn * hn, axis=-1, keepdims=True) + RMS_EPS) * g_ref[...]
    xn_ref[...] = y.astype(xn_ref.dtype)


def _out_stage(oa, ob, oc, plain, h, wa, wb, wc, wo, g_next, off_gates, xn_dtype):
    tp, d = h.shape
    tm = _pick(tp, (256, 128))
    bg = off_gates // d
    row = lambda w, cb=0: pl.BlockSpec((tm, w), lambda i: (i, cb))
    full = lambda a: pl.BlockSpec(a.shape, lambda i: (0, 0))
    return pl.pallas_call(
        _out_kernel,
        grid=(tp // tm,),
        in_specs=[row(oa.shape[1]), row(ob.shape[1]), row(oc.shape[1]),
                  row(d, bg), row(d, bg + 1), row(d, bg + 2), row(d),
                  full(wa), full(wb), full(wc), full(wo), pl.BlockSpec((1, d), lambda i: (0, 0))],
        out_specs=[row(d), row(d)],
        out_shape=[jax.ShapeDtypeStruct((tp, d), F32), jax.ShapeDtypeStruct((tp, d), xn_dtype)],
        compiler_params=_cparams("parallel"),
        name="merge_out",
    )(oa, ob, oc, plain, plain, plain, h, wa, wb, wc, wo, g_next.reshape(1, d))


def kernel(x, meta_tokens, norm_g, w_in, gdn_conv_w, gdn_a_log, gdn_dt_bias, gdn_norm_g, ssm_conv_w,
           ssm_conv_b, ssm_a_log, ssm_dt_bias, ssm_d, ssm_norm_g, w_branch_a, w_branch_b, w_branch_c,
           w_out, final_norm_g):
    bsz, seq, d = x.shape
    depth = norm_g.shape[0]
    sbw = w_branch_a.shape[1]
    vw = w_branch_b.shape[1]
    inner = w_branch_c.shape[1]
    hgdn = gdn_a_log.shape[1]
    hssm = ssm_a_log.shape[1]
    conv_ch = ssm_conv_w.shape[2]
    qkw = (gdn_conv_w.shape[2] - vw) // 2
    hsb = sbw // HEAD_DIM
    assert sbw == d and vw == d and qkw == hgdn * HEAD_DIM and vw == hgdn * HEAD_DIM
    assert inner == hssm * SSM_P and conv_ch == inner + 2 * SSM_GROUPS * SSM_STATE
    assert 2 * hgdn + hssm <= LANE and (inner // SSM_GROUPS) % (2 * SSM_P) == 0

    lreal = N_META + seq
    lp = -(-(lreal + CONV_K - 1) // ATT_BLOCK) * ATT_BLOCK
    pad = lp - lreal
    assert (lp // 2) % CHUNK == 0

    o_sb = 0
    o_gq = 4 * sbw
    o_gz = o_gq + 2 * qkw + vw
    o_gb = o_gz + vw
    o_ga = o_gb + hgdn
    o_sz = o_ga + hgdn
    o_sx = o_sz + inner
    o_dt = o_sx + conv_ch
    o_gt = o_dt + hssm
    assert w_in.shape[2] == o_gt + 3 * d
    w_sb, w_gz, w_gt, w_sz = (w_in[:, :, a:b].astype(BF16)
                              for a, b in ((o_sb, o_gq), (o_gz, o_gb), (o_gt, o_gt + 3 * d), (o_sz, o_sx)))
    w_gconv = w_in[:, :, o_gq:o_gz].astype(BF16)
    w_sconv = w_in[:, :, o_sx:o_dt].astype(BF16)
    nsmall = 2 * hgdn + hssm
    w_small = jnp.concatenate([w_in[:, :, o_gb:o_sz], w_in[:, :, o_dt:o_gt],
                               jnp.zeros((depth, d, LANE - nsmall), F32)], axis=2).astype(BF16)
    zpad = lambda t: jnp.concatenate([t, jnp.zeros((depth, LANE - t.shape[1]), F32)], axis=1)
    kind = jnp.concatenate([jnp.zeros((hgdn,), F32), jnp.ones((hgdn,), F32), jnp.full((hssm,), 2.0, F32),
                            jnp.full((LANE - nsmall,), 3.0, F32)])
    small_par = jnp.stack([
        jnp.broadcast_to(kind, (depth, LANE)),
        zpad(jnp.concatenate([jnp.zeros((depth, hgdn), F32), gdn_a_log], axis=1)),
        zpad(jnp.concatenate([jnp.zeros((depth, hgdn), F32), gdn_dt_bias, ssm_dt_bias], axis=1)),
    ] + [jnp.zeros((depth, LANE), F32)] * 5, axis=1)
    wa, wb, wc, wo = (t.astype(BF16) for t in (w_branch_a, w_branch_b, w_branch_c, w_out))
    gconv_b = jnp.zeros((1, w_gconv.shape[2]), F32)

    meta = jnp.broadcast_to(meta_tokens.astype(x.dtype)[None], (bsz, N_META, d))
    h = jnp.concatenate([jnp.zeros((bsz, pad, d), x.dtype), meta, x], axis=1).reshape(bsz * lp, d)
    xn = _rmsnorm(h, norm_g[0], BF16)

    plain = lambda w, name: _inproj(xn, w, bsz, lp, tn=_pick(w.shape[1], (1024, 512, 256)), kernel=_inproj_plain_kernel,
                                    name=name)
    tn_g = _pick(qkw, (256, 128))
    tn_s = _pick(conv_ch, (256, 128))
    conv_specs = lambda tn: (pl.BlockSpec((CONV_K, tn), lambda b, j: (0, j)),
                             pl.BlockSpec((1, tn), lambda b, j: (0, j)))
    for layer in range(depth):
        sb = plain(w_sb[layer], "inproj_sb")
        gdz = plain(w_gz[layer], "inproj_gdn_z")
        gates = plain(w_gt[layer], "inproj_gates")
        ssz = plain(w_sz[layer], "inproj_ssm_z")
        gconv = _inproj(xn, w_gconv[layer], bsz, lp, tn=tn_g,
                        kernel=functools.partial(_inproj_conv_kernel, pad=pad, norm_blocks=qkw // tn_g,
                                                 q_scale=HEAD_DIM ** -0.5),
                        extra=(gdn_conv_w[layer], gconv_b), extra_specs=conv_specs(tn_g), name="inproj_gdn_conv")
        sconv = _inproj(xn, w_sconv[layer], bsz, lp, tn=tn_s,
                        kernel=functools.partial(_inproj_conv_kernel, pad=pad, norm_blocks=0, q_scale=1.0),
                        extra=(ssm_conv_w[layer], ssm_conv_b[layer].reshape(1, conv_ch)),
                        extra_specs=conv_specs(tn_s), name="inproj_ssm_conv")
        small = _inproj(xn, w_small[layer], bsz, lp, tn=LANE,
                        kernel=functools.partial(_inproj_small_kernel, pad=pad),
                        extra=(small_par[layer],), extra_specs=(pl.BlockSpec((8, LANE), lambda b, j: (0, 0)),),
                        name="inproj_small")
        oa = _attention(sb, bsz, lp, hsb, 0, sbw, 2 * sbw, 3 * sbw)
        ob = _gdn(gconv, small, gdz, gdn_norm_g[layer], bsz, lp, pad, hgdn, 0)
        oc = _ssd(sconv, small, ssz, ssm_a_log[layer], ssm_d[layer], ssm_norm_g[layer], bsz, lp, pad, hgdn,
                  inner, 0)
        last = layer == depth - 1
        g_next = final_norm_g if last else norm_g[layer + 1]
        h, xn = _out_stage(oa, ob, oc, gates, h, wa[layer], wb[layer], wc[layer], wo[layer], g_next, 0,
                           F32 if last else BF16)
    return xn.reshape(bsz, lp, d)[:, pad + N_META:]
```

```python
import functools

import jax
import jax.numpy as jnp
from jax import lax
from jax.experimental import pallas as pl
from jax.experimental.pallas import tpu as pltpu

F32 = jnp.float32
BF16 = jnp.bfloat16

N_META = 16
RMS_EPS = 1e-6
L2_EPS = 1e-6
CONV_K = 4
HEAD_DIM = 128
SSM_GROUPS = 2
SSM_STATE = 128
SSM_P = 64
CHUNK = 64
LANE = 128
ATT_BLOCK = 256
VMEM_LIMIT = 52 * 1024 * 1024
LOG2E = 1.4426950408889634
EXP2_CAP = 126.0
GDN_UNROLL = 18
SSD_SPLIT = 1
GDN_HEADS = 2
ATT_HEADS = 4
ATT_LOG_TERMS = 1


def _cparams(*sem):
    return pltpu.CompilerParams(dimension_semantics=sem, vmem_limit_bytes=VMEM_LIMIT)


def _sigmoid(x):
    return 1.0 / (1.0 + jnp.exp(-x))


def _silu(x):
    return x * _sigmoid(x)


def _softplus(x):
    return jnp.maximum(x, 0.0) + jnp.log1p(jnp.exp(-jnp.abs(x)))


def _split3(x):
    hi = x.astype(BF16)
    r = x - hi.astype(F32)
    mid = r.astype(BF16)
    lo = (r - mid.astype(F32)).astype(BF16)
    return hi, mid, lo


def _dot(a, b):
    return jnp.dot(a, b, preferred_element_type=F32)


def _dot_nt(a, b):
    return lax.dot_general(a, b, (((1,), (1,)), ((), ())), preferred_element_type=F32)


def _dot_tn(a, b):
    return lax.dot_general(a, b, (((0,), (0,)), ((), ())), preferred_element_type=F32)


def _dot_hp(a, b):
    ah = a.astype(BF16)
    al = (a - ah.astype(F32)).astype(BF16)
    bh = b.astype(BF16)
    bl = (b - bh.astype(F32)).astype(BF16)
    return _dot(ah, bh) + (_dot(al, bh) + _dot(ah, bl))


def _pick(n, cands):
    for c in cands:
        if n % c == 0:
            return c
    raise ValueError(f"no tile for {n} in {cands}")


def _rmsnorm_kernel(h_ref, g_ref, o_ref):
    x = h_ref[...]
    y = x * lax.rsqrt(jnp.mean(x * x, axis=-1, keepdims=True) + RMS_EPS) * g_ref[...]
    o_ref[...] = y.astype(o_ref.dtype)


def _rmsnorm(h, g, out_dtype):
    tp, d = h.shape
    tm = _pick(tp, (512, 384, 256, 128))
    return pl.pallas_call(
        _rmsnorm_kernel,
        grid=(tp // tm,),
        in_specs=[pl.BlockSpec((tm, d), lambda i: (i, 0)), pl.BlockSpec((1, d), lambda i: (0, 0))],
        out_specs=pl.BlockSpec((tm, d), lambda i: (i, 0)),
        out_shape=jax.ShapeDtypeStruct((tp, d), out_dtype),
        compiler_params=_cparams("parallel"),
        name="rmsnorm",
    )(h, g.reshape(1, d))


def _inproj_plain_kernel(x_ref, w_ref, o_ref):
    o_ref[...] = _dot(x_ref[...], w_ref[...]).astype(o_ref.dtype)


def _inproj_conv_kernel(x_ref, w_ref, cw_ref, cb_ref, o_ref, *, pad, norm_blocks, q_scale):
    acc = _dot(x_ref[...], w_ref[...])
    cw = cw_ref[...]
    y = acc * cw[CONV_K - 1:CONV_K, :]
    for k in range(CONV_K - 1):
        y = y + pltpu.roll(acc, CONV_K - 1 - k, axis=0) * cw[k:k + 1, :]
    y = _silu(y + cb_ref[...])
    if norm_blocks:
        j = pl.program_id(1)
        for s in range(y.shape[1] // HEAD_DIM):
            ys = y[:, s * HEAD_DIM:(s + 1) * HEAD_DIM]
            rs = lax.rsqrt(jnp.sum(ys * ys, axis=1, keepdims=True) + L2_EPS)
            fac = jnp.where(j < norm_blocks, rs * q_scale, jnp.where(j < 2 * norm_blocks, rs, 1.0))
            o_ref[:, s * HEAD_DIM:(s + 1) * HEAD_DIM] = (ys * fac).astype(o_ref.dtype)
    else:
        o_ref[...] = y.astype(o_ref.dtype)
    o_ref[0:pad, :] = jnp.zeros((pad, o_ref.shape[1]), o_ref.dtype)


def _inproj_small_kernel(x_ref, w_ref, p_ref, o_ref, *, pad):
    v = _dot(x_ref[...], w_ref[...])
    kind = p_ref[0:1, :]
    alog = p_ref[1:2, :]
    bias = p_ref[2:3, :]
    sp = _softplus(v + bias)
    out = jnp.where(kind == 0.0, _sigmoid(v),
                    jnp.where(kind == 1.0, -jnp.exp(alog) * sp, jnp.where(kind == 2.0, sp, 0.0)))
    row = lax.broadcasted_iota(jnp.int32, out.shape, 0)
    o_ref[...] = jnp.where(row >= pad, out, 0.0)


def _inproj(xn, w, bsz, lp, *, tn, kernel, extra=(), extra_specs=(), out_dtype=F32, name):
    d = xn.shape[1]
    n = w.shape[1]
    return pl.pallas_call(
        kernel,
        grid=(bsz, n // tn),
        in_specs=[pl.BlockSpec((lp, d), lambda b, j: (b, 0)),
                  pl.BlockSpec((d, tn), lambda b, j: (0, j)), *extra_specs],
        out_specs=pl.BlockSpec((lp, tn), lambda b, j: (b, j)),
        out_shape=jax.ShapeDtypeStruct((bsz * lp, n), out_dtype),
        compiler_params=_cparams("parallel", "arbitrary"),
        name=name,
    )(xn, w, *extra)


def _attn_kernel(q_ref, k_ref, v_ref, z_ref, uu_ref, o_ref, *, blk, nhb, scale):
    i = pl.program_id(2)
    uu = uu_ref[...]
    causal = (lax.broadcasted_iota(jnp.int32, (blk, blk), 1) < lax.broadcasted_iota(jnp.int32, (blk, blk), 0))
    qs = [(q_ref[:, hh * HEAD_DIM:(hh + 1) * HEAD_DIM] * (-scale * LOG2E)).astype(BF16) for hh in range(nhb)]

    def logits(hh, j):
        r0 = pl.multiple_of(j * blk, blk)
        kb = k_ref[pl.ds(r0, blk), hh * HEAD_DIM:(hh + 1) * HEAD_DIM].astype(BF16)
        return _dot_nt(qs[hh], kb)

    def log_terms(zn, diag):
        zn = jnp.minimum(zn, EXP2_CAP)
        sp = jnp.log2(1.0 + jnp.exp2(zn))
        lk = zn - sp
        if diag:
            lk = jnp.where(causal, lk, 0.0)
        terms, resid = [], lk
        for t in range(ATT_LOG_TERMS):
            terms.append(resid.astype(BF16))
            if t + 1 < ATT_LOG_TERMS:
                resid = resid - terms[-1].astype(F32)
        return sp, lk, jnp.concatenate(terms, axis=1) if len(terms) > 1 else terms[0]

    def weights(sp, later, diag):
        w = jnp.exp2(later - sp)
        if diag:
            w = jnp.where(causal, w, 0.0)
        return w.astype(BF16)

    def values(hh, j):
        r0 = pl.multiple_of(j * blk, blk)
        return v_ref[pl.ds(r0, blk), hh * HEAD_DIM:(hh + 1) * HEAD_DIM].astype(BF16)

    def tiles(j, acc, run, diag):
        heads = range(nhb)
        zs = [logits(hh, j) for hh in heads]
        lt = [log_terms(z, diag) for z in zs]
        later = [_dot(lt[hh][2], uu) + run[hh * blk:(hh + 1) * blk] for hh in heads]
        ws = [weights(lt[hh][0], later[hh], diag) for hh in heads]
        pv = [_dot(ws[hh], values(hh, j)) for hh in heads]
        rs = [jnp.sum(lt[hh][1], axis=1, keepdims=True) for hh in heads]
        return acc + jnp.concatenate(pv, axis=0), run + jnp.concatenate(rs, axis=0)

    carry = tiles(i, jnp.zeros((nhb * blk, HEAD_DIM), F32), jnp.zeros((nhb * blk, 1), F32), True)
    acc, _ = lax.fori_loop(1, i + 1, lambda step, c: tiles(i - step, *c, False), carry)
    for hh in range(nhb):
        sl = slice(hh * HEAD_DIM, (hh + 1) * HEAD_DIM)
        o_ref[:, sl] = (acc[hh * blk:(hh + 1) * blk] * _silu(z_ref[:, sl].astype(F32))).astype(o_ref.dtype)


def _attention(qarr, kvz, bsz, lp, nh, off_q, off_k, off_v, off_z):
    blk = ATT_BLOCK
    nq = lp // blk
    nhb = next(n for n in (ATT_HEADS, 2, 1) if nh % n == 0)
    wd = nhb * HEAD_DIM
    j = lax.broadcasted_iota(jnp.int32, (ATT_LOG_TERMS * blk, blk), 0) % blk
    s = lax.broadcasted_iota(jnp.int32, (ATT_LOG_TERMS * blk, blk), 1)
    uu = (j > s).astype(BF16)
    bq, bk, bv, bz = (o // wd for o in (off_q, off_k, off_v, off_z))
    return pl.pallas_call(
        functools.partial(_attn_kernel, blk=blk, nhb=nhb, scale=HEAD_DIM ** -0.5),
        grid=(bsz, nh // nhb, nq),
        in_specs=[pl.BlockSpec((blk, wd), lambda b, h, i: (b * nq + i, bq + h)),
                  pl.BlockSpec((lp, wd), lambda b, h, i: (b, bk + h)),
                  pl.BlockSpec((lp, wd), lambda b, h, i: (b, bv + h)),
                  pl.BlockSpec((blk, wd), lambda b, h, i: (b * nq + i, bz + h)),
                  pl.BlockSpec((ATT_LOG_TERMS * blk, blk), lambda b, h, i: (0, 0))],
        out_specs=pl.BlockSpec((blk, wd), lambda b, h, i: (b * nq + i, h)),
        out_shape=jax.ShapeDtypeStruct((bsz * lp, nh * HEAD_DIM), BF16),
        compiler_params=_cparams("parallel", "parallel", "arbitrary"),
        name="sb_attention",
    )(qarr, kvz, kvz, kvz, uu)


def _gdn_kernel(q_ref, k_ref, v_ref, sm_ref, z_ref, ng_ref, ll3_ref, wx_ref, o_ref,
                a_s, b_s, qp_s, ob_s, eg_s, *, lp, pad, nh, nhb, unroll):
    h0 = pl.program_id(1) * nhb
    cl = CHUNK
    nc = lp // cl
    c0 = pad // cl
    c1 = nc - (nc - c0 + unroll - 1) // unroll * unroll
    assert c1 >= 0
    ll3 = ll3_ref[...]
    wx = (wx_ref[0:cl, :], wx_ref[cl:2 * cl, :])
    li = lax.broadcasted_iota(jnp.int32, (cl, 2 * cl), 0)
    lanew = lax.broadcasted_iota(jnp.int32, (cl, 2 * cl), 1)
    first = lanew < cl
    si = lanew & (cl - 1)
    eye = (li == si).astype(F32)
    lane = lax.broadcasted_iota(jnp.int32, (cl, LANE), 1)
    zero_b = jnp.zeros((cl, 2 * cl), BF16)
    zero_h = jnp.zeros((cl, HEAD_DIM), BF16)
    if c1:
        ob_s[:, 0:c1 * cl, :] = jnp.zeros((nhb, c1 * cl, HEAD_DIM), F32)

    def split2(x):
        hi = x.astype(BF16)
        return hi, (x - hi.astype(F32)).astype(BF16)

    def blockdiag(xb):
        return jnp.concatenate([jnp.where(first, xb, zero_b), jnp.where(first, zero_b, xb)], axis=0)

    def pair_mul(x, bd):
        xh, xl = split2(x)
        return _dot(jnp.concatenate([xh, xl, xh], axis=1), jnp.concatenate([bd[0], bd[0], bd[1]], axis=0))

    def rhs_pair(a0, a1, b0, b1):
        return jnp.concatenate([jnp.concatenate([a0, a1, zero_h, zero_h], axis=1),
                                jnp.concatenate([zero_h, zero_h, b0, b1], axis=1)], axis=0)

    def pass1(hh, i):
        c = c1 + i * unroll
        cs = range(unroll)
        ps = range(unroll // 2)
        hs = slice(hh * HEAD_DIM, (hh + 1) * HEAD_DIM)
        rows = [pl.ds(pl.multiple_of((c + r) * cl, cl), cl) for r in cs]
        q = [q_ref[rw, hs] for rw in rows]
        k = [k_ref[rw, hs] for rw in rows]
        v = [v_ref[rw, hs] for rw in rows]
        sm = [sm_ref[rw, :] for rw in rows]
        beta = [jnp.sum(jnp.where(lane == h0 + hh, x, 0.0), axis=1, keepdims=True) for x in sm]
        g = [jnp.sum(jnp.where(lane == nh + h0 + hh, x, 0.0), axis=1, keepdims=True) for x in sm]
        sg = [_dot(ll3, jnp.concatenate(_split3(g[r] * wx[r % 2]), axis=0)) for r in cs]
        eseg = [jnp.exp(sg[2 * p][0:cl, 0:2 * cl] + sg[2 * p + 1][0:cl, 0:2 * cl]) for p in ps]
        gcb = [x[0:cl, 2 * cl:4 * cl] for x in sg]
        glb = [x[cl:2 * cl, 2 * cl:4 * cl] for x in sg]
        egc = [jnp.exp(x) for x in gcb]
        kb = [k[r] * beta[r] for r in cs]
        kq = [_dot_nt(jnp.concatenate([kb[2 * p], kb[2 * p + 1], q[2 * p], q[2 * p + 1]], axis=0),
                      jnp.concatenate([k[2 * p], k[2 * p + 1]], axis=0)) for p in ps]
        kk = [jnp.where(first, x[0:cl], x[cl:2 * cl]) for x in kq]
        aqk = [(jnp.where(first, kq[p][2 * cl:3 * cl], kq[p][3 * cl:4 * cl])
                * jnp.where(li >= si, eseg[p], 0.0)).astype(BF16) for p in ps]
        pw = [-(kk[p] * jnp.where(li > si, eseg[p], 0.0)) for p in ps]
        tw = [eye + x for x in pw]
        bd = [tuple(blockdiag(y) for y in split2(x)) for x in pw]
        pw = [pair_mul(pw[p], bd[p]) for p in ps]
        for rnd in range(5):
            bd = [tuple(blockdiag(y) for y in split2(x)) for x in pw]
            if rnd < 4:
                both = [pair_mul(jnp.concatenate([pw[p], tw[p]], axis=0), bd[p]) for p in ps]
                pw = [x[0:cl] for x in both]
                tw = [tw[p] + both[p][cl:2 * cl] for p in ps]
            else:
                tw = [tw[p] + pair_mul(tw[p], bd[p]) for p in ps]
        vb = [(v[r] * beta[r]).astype(BF16) for r in cs]
        kbe = [(kb[r] * egc[r]).astype(BF16) for r in cs]
        uw = [_dot(tw[p].astype(BF16), rhs_pair(kbe[2 * p], vb[2 * p], kbe[2 * p + 1], vb[2 * p + 1])) for p in ps]
        wu = [uw[r // 2][:, (r % 2) * 2 * HEAD_DIM:(r % 2 + 1) * 2 * HEAD_DIM] for r in cs]
        wub = [x.astype(BF16) for x in wu]
        awu = [_dot(aqk[p], rhs_pair(wub[2 * p][:, 0:HEAD_DIM], wub[2 * p][:, HEAD_DIM:],
                                     wub[2 * p + 1][:, 0:HEAD_DIM], wub[2 * p + 1][:, HEAD_DIM:])) for p in ps]
        ke = [k[r] * jnp.exp(glb[r] - gcb[r]) for r in cs]
        ab = [_dot_tn(ke[r].astype(BF16), wub[r]) for r in cs]
        aw = [awu[r // 2][:, (r % 2) * 2 * HEAD_DIM:(r % 2) * 2 * HEAD_DIM + HEAD_DIM] for r in cs]
        au = [awu[r // 2][:, (r % 2) * 2 * HEAD_DIM + HEAD_DIM:(r % 2 + 1) * 2 * HEAD_DIM] for r in cs]
        outs = ([x[:, 0:HEAD_DIM].astype(a_s.dtype) for x in ab], [x[:, HEAD_DIM:] for x in ab],
                [(q[r] * egc[r] - aw[r]).astype(qp_s.dtype) for r in cs], au,
                [jnp.exp(x[0:8, :]) for x in glb])
        for ref, nr, vals in zip((a_s, b_s, qp_s, ob_s, eg_s), (HEAD_DIM, HEAD_DIM, cl, cl, 8), outs):
            ref[hh, pl.ds(pl.multiple_of(c * nr, nr), unroll * nr), :] = jnp.concatenate(vals, axis=0)

    for hh in range(nhb):
        lax.fori_loop(0, (nc - c1) // unroll, lambda i, carry, hh=hh: (pass1(hh, i), carry)[1], 0)

    def pass2(c, states):
        heads = range(nhb)
        r0 = pl.ds(pl.multiple_of(c * cl, cl), cl)
        rk = pl.ds(pl.multiple_of(c * HEAD_DIM, HEAD_DIM), HEAD_DIM)
        r8 = pl.ds(pl.multiple_of(c * 8, 8), 8)
        lhs = [jnp.concatenate([a_s[hh, rk, :], qp_s[hh, r0, :]], axis=0) for hh in heads]
        mm = [_dot(lhs[hh], states[hh].astype(BF16)) for hh in heads]
        for hh in heads:
            ob_s[hh, r0, :] = ob_s[hh, r0, :] + mm[hh][HEAD_DIM:, :]
        return tuple(states[hh] * eg_s[hh, r8, :][0:1, :] - mm[hh][0:HEAD_DIM, :] + b_s[hh, rk, :] for hh in heads)

    lax.fori_loop(c0, nc, pass2, tuple(jnp.zeros((HEAD_DIM, HEAD_DIM), F32) for _ in range(nhb)))

    for hh in range(nhb):
        hs = slice(hh * HEAD_DIM, (hh + 1) * HEAD_DIM)
        o = ob_s[hh]
        on = o * lax.rsqrt(jnp.mean(o * o, axis=1, keepdims=True) + RMS_EPS) * ng_ref[...]
        o_ref[:, hs] = (on * _silu(z_ref[:, hs].astype(F32))).astype(o_ref.dtype)


def _gdn(gconv, small, plain, ng, bsz, lp, pad, nh, off_z):
    cl = CHUNK
    nc = lp // cl
    n = nc - pad // cl
    unroll = next(u for u in range(min(GDN_UNROLL, nc), 0, -2) if -(-n // u) * u <= nc)
    assert unroll % 2 == 0
    l = lax.broadcasted_iota(jnp.int32, (2 * cl, 3 * cl), 0)
    jj = lax.broadcasted_iota(jnp.int32, (2 * cl, 3 * cl), 1) % cl
    ll3 = ((l >= cl) | (jj <= l)).astype(BF16)
    j2 = lax.broadcasted_iota(jnp.int32, (2 * cl, 4 * cl), 0)
    s2 = lax.broadcasted_iota(jnp.int32, (2 * cl, 4 * cl), 1)
    tri = (s2 // cl == j2 // cl) & (j2 % cl > s2 % cl)
    wx = jnp.where(s2 < 2 * cl, tri.astype(F32), 1.0)
    nhb = next(m for m in (GDN_HEADS, 1) if nh % m == 0)
    wd = nhb * HEAD_DIM
    bz = off_z // wd
    nb = nh // nhb
    seq = lambda off: pl.BlockSpec((lp, wd), lambda b, h: (b, off + h))
    const = lambda shp: pl.BlockSpec(shp, lambda b, h: (0, 0))
    return pl.pallas_call(
        functools.partial(_gdn_kernel, lp=lp, pad=pad, nh=nh, nhb=nhb, unroll=unroll),
        grid=(bsz, nb),
        in_specs=[seq(0), seq(nb), seq(2 * nb),
                  pl.BlockSpec((lp, LANE), lambda b, h: (b, 0)),
                  seq(bz), const((1, HEAD_DIM)), const((2 * cl, 3 * cl)), const((2 * cl, 4 * cl))],
        out_specs=pl.BlockSpec((lp, wd), lambda b, h: (b, h)),
        out_shape=jax.ShapeDtypeStruct((bsz * lp, nh * HEAD_DIM), BF16),
        scratch_shapes=[pltpu.VMEM((nhb, nc * HEAD_DIM, HEAD_DIM), BF16),
                        pltpu.VMEM((nhb, nc * HEAD_DIM, HEAD_DIM), F32),
                        pltpu.VMEM((nhb, lp, HEAD_DIM), BF16), pltpu.VMEM((nhb, lp, HEAD_DIM), F32),
                        pltpu.VMEM((nhb, nc * 8, HEAD_DIM), F32)],
        compiler_params=_cparams("parallel", "parallel"),
        name="gated_deltanet",
    )(gconv, gconv, gconv, small, plain, ng.reshape(1, HEAD_DIM), ll3, wx)


def _ssd_kernel(x_ref, b_ref, c_ref, sm_ref, z_ref, esel_ref, alog_ref, d_ref, ng_ref, ll_ref, wt_ref,
                incl_ref, o_ref, s_ref, *, rows, hg, skip):
    cl = CHUNK
    wd = hg * SSM_P
    groups = range(SSM_GROUPS)

    @pl.when(pl.program_id(1) == 0)
    def _():
        s_ref[...] = jnp.zeros_like(s_ref)

    wu = wd // SSD_SPLIT
    ll = ll_ref[...]
    wt = wt_ref[:, 0:wu]
    incl = incl_ref[:, 0:wu]
    lane = lax.broadcasted_iota(jnp.int32, (cl, LANE), 1)
    zero_b = jnp.zeros((cl, LANE), BF16)

    def split2(x):
        hi = x.astype(BF16)
        return hi, (x - hi.astype(F32)).astype(BF16)

    def walk(c, units):
        rw = pl.ds(pl.multiple_of(c * cl, cl), cl)
        us = range(len(units))
        gof = [g for g, _ in units]
        ls = [slice(hf * wu, (hf + 1) * wu) for _, hf in units]
        gl = [slice(g * wd + hf * wu, g * wd + (hf + 1) * wu) for g, hf in units]
        x = [x_ref[rw, gl[u]] for u in us]
        bm = [b_ref[rw, g * SSM_STATE:(g + 1) * SSM_STATE] for g in groups]
        cm = [c_ref[rw, g * SSM_STATE:(g + 1) * SSM_STATE] for g in groups]
        sm2 = jnp.concatenate(split2(sm_ref[rw, :]), axis=1)
        dt = [_dot(sm2, esel_ref[gof[u], :, ls[u]]) for u in us]
        la = [dt[u] * -jnp.exp(alog_ref[gof[u], :, ls[u]]) for u in us]
        xs = [x[u] * dt[u] for u in us]
        lab = [split2(la[u]) for u in us]
        sg = [_dot(ll, jnp.concatenate([jnp.concatenate([t * wt, t], axis=1) for t in lab[u]], axis=0))
              for u in us]
        decay = [jnp.where(incl > 0.0, jnp.exp(sg[u][0:cl, 0:wu]), 0.0) for u in us]
        cs = [sg[u][0:cl, wu:2 * wu] for u in us]
        tot = [sg[u][cl:2 * cl, wu:2 * wu] for u in us]
        btile = [jnp.concatenate([bm[g]] * (wu // cl), axis=0) for g in groups]
        scores = [_dot_nt(cm[gof[u]], btile[gof[u]]) for u in us]
        mm = [(scores[u] * decay[u]).astype(BF16) for u in us]
        xb = [xs[u].astype(BF16) for u in us]
        y_diag = []
        for u in us:
            ys = []
            for pp in range(wu // LANE):
                xp = xb[u][:, pp * LANE:(pp + 1) * LANE]
                bd = jnp.concatenate([jnp.where(lane < SSM_P, xp, zero_b), jnp.where(lane < SSM_P, zero_b, xp)],
                                     axis=0)
                ys.append(_dot(mm[u][:, pp * LANE:(pp + 1) * LANE], bd))
            y_diag.append(jnp.concatenate(ys, axis=1))
        state = [s_ref[gof[u], :, ls[u]] for u in us]
        y_off = [_dot(cm[gof[u]], state[u]) * jnp.exp(cs[u]) for u in us]
        ds = [_dot_tn(bm[gof[u]], xs[u] * jnp.exp(tot[u] - cs[u])) for u in us]
        for u in us:
            s_ref[gof[u], :, ls[u]] = state[u] * jnp.exp(tot[u][0:1, :]) + ds[u]
        y = [(y_diag[u] + y_off[u] + d_ref[gof[u], :, ls[u]] * x[u]) * _silu(z_ref[rw, gl[u]].astype(F32))
             for u in us]
        sq = [jnp.sum(y[u] * y[u], axis=1, keepdims=True) for u in us]
        for g in sorted(set(gof)):
            rs = lax.rsqrt(sum(sq[u] for u in us if gof[u] == g) * (1.0 / wd) + RMS_EPS)
            for u in us:
                if gof[u] == g:
                    o_ref[rw, gl[u]] = (y[u] * rs * ng_ref[g, :, ls[u]]).astype(o_ref.dtype)

    def body(c, carry):
        walk(c, [(g, hf) for g in groups for hf in range(SSD_SPLIT)])
        return carry

    first = pl.program_id(1) == 0
    if skip:
        @pl.when(first)
        def _():
            o_ref[0:skip * cl, :] = jnp.zeros((skip * cl, o_ref.shape[1]), o_ref.dtype)

    lax.fori_loop(jnp.where(first, skip, 0), rows // cl, body, 0)


def _ssd(sconv, small, plain, alog, dvec, ng, bsz, lp, pad, n_gdn_heads, inner, off_z):
    cl = CHUNK
    ng_ = SSM_GROUPS
    wd = inner // ng_
    hg = wd // SSM_P
    ns = 4
    rows = lp // ns
    assert rows % cl == 0 and off_z % inner == 0 and inner % (ng_ * SSM_STATE) == 0
    col = lax.broadcasted_iota(jnp.int32, (ng_, 2 * LANE, wd), 1) % LANE
    hh = lax.broadcasted_iota(jnp.int32, (ng_, 2 * LANE, wd), 2) // SSM_P
    gi = lax.broadcasted_iota(jnp.int32, (ng_, 2 * LANE, wd), 0)
    esel = (col == 2 * n_gdn_heads + gi * hg + hh).astype(BF16)
    l = lax.broadcasted_iota(jnp.int32, (2 * cl, 2 * cl), 0)
    jj = lax.broadcasted_iota(jnp.int32, (2 * cl, 2 * cl), 1) % cl
    ll = ((l >= cl) | (jj <= l)).astype(BF16)
    j2 = lax.broadcasted_iota(jnp.int32, (cl, wd), 0)
    s2 = lax.broadcasted_iota(jnp.int32, (cl, wd), 1) % cl
    wt = (j2 > s2).astype(BF16)
    incl = (s2 <= j2).astype(F32)
    expand = lambda t: jnp.repeat(t.reshape(ng_, 1, hg), SSM_P, axis=2)
    nbc = ng_ * SSM_STATE
    rowblk = lambda w, cb: pl.BlockSpec((rows, w), lambda b, s: (b * ns + s, cb))
    full = lambda shp: pl.BlockSpec(shp, lambda b, s: (0,) * len(shp))
    return pl.pallas_call(
        functools.partial(_ssd_kernel, rows=rows, hg=hg, skip=min(pad // cl, rows // cl)),
        grid=(bsz, ns),
        in_specs=[rowblk(inner, 0), rowblk(nbc, inner // nbc), rowblk(nbc, inner // nbc + 1), rowblk(LANE, 0),
                  rowblk(inner, off_z // inner),
                  full((ng_, 2 * LANE, wd)), full((ng_, 1, wd)), full((ng_, 1, wd)), full((ng_, 1, wd)),
                  full((2 * cl, 2 * cl)), full((cl, wd)), full((cl, wd))],
        out_specs=rowblk(inner, 0),
        out_shape=jax.ShapeDtypeStruct((bsz * lp, inner), BF16),
        scratch_shapes=[pltpu.VMEM((ng_, SSM_STATE, wd), F32)],
        compiler_params=_cparams("parallel", "arbitrary"),
        name="ssd",
    )(sconv, sconv, sconv, small, plain, esel, expand(alog), expand(dvec), ng.reshape(ng_, 1, wd),
      ll, wt, incl)


def _out_kernel(oa_ref, ob_ref, oc_ref, ga_ref, gb_ref, gc_ref, h_ref, wa_ref, wb_ref, wc_ref, wo_ref,
                g_ref, hn_ref, xn_ref):
    merged = (_sigmoid(ga_ref[...].astype(F32)) * _dot(oa_ref[...], wa_ref[...])
              + _sigmoid(gb_ref[...].astype(F32)) * _dot(ob_ref[...], wb_ref[...])
              + _sigmoid(gc_ref[...].astype(F32)) * _dot(oc_ref[...], wc_ref[...]))
    hn = h_ref[...] + _dot(merged.astype(BF16), wo_ref[...])
    hn_ref[...] = hn
    y = hn * lax.rsqrt(jnp.mean(hn * hn, axis=-1, keepdims=True) + RMS_EPS) * g_ref[...]
    xn_ref[...] = y.astype(xn_ref.dtype)


def _out_stage(oa, ob, oc, plain, h, wa, wb, wc, wo, g_next, off_gates, xn_dtype):
    tp, d = h.shape
    tm = _pick(tp, (256, 128))
    bg = off_gates // d
    row = lambda w, cb=0: pl.BlockSpec((tm, w), lambda i: (i, cb))
    full = lambda a: pl.BlockSpec(a.shape, lambda i: (0, 0))
    return pl.pallas_call(
        _out_kernel,
        grid=(tp // tm,),
        in_specs=[row(oa.shape[1]), row(ob.shape[1]), row(oc.shape[1]),
                  row(d, bg), row(d, bg + 1), row(d, bg + 2), row(d),
                  full(wa), full(wb), full(wc), full(wo), pl.BlockSpec((1, d), lambda i: (0, 0))],
        out_specs=[row(d), row(d)],
        out_shape=[jax.ShapeDtypeStruct((tp, d), F32), jax.ShapeDtypeStruct((tp, d), xn_dtype)],
        compiler_params=_cparams("parallel"),
        name="merge_out",
    )(oa, ob, oc, plain, plain, plain, h, wa, wb, wc, wo, g_next.reshape(1, d))


def kernel(x, meta_tokens, norm_g, w_in, gdn_conv_w, gdn_a_log, gdn_dt_bias, gdn_norm_g, ssm_conv_w,
           ssm_conv_b, ssm_a_log, ssm_dt_bias, ssm_d, ssm_norm_g, w_branch_a, w_branch_b, w_branch_c,
           w_out, final_norm_g):
    bsz, seq, d = x.shape
    depth = norm_g.shape[0]
    sbw = w_branch_a.shape[1]
    vw = w_branch_b.shape[1]
    inner = w_branch_c.shape[1]
    hgdn = gdn_a_log.shape[1]
    hssm = ssm_a_log.shape[1]
    conv_ch = ssm_conv_w.shape[2]
    qkw = (gdn_conv_w.shape[2] - vw) // 2
    hsb = sbw // HEAD_DIM
    assert sbw == d and vw == d and qkw == hgdn * HEAD_DIM and vw == hgdn * HEAD_DIM
    assert inner == hssm * SSM_P and conv_ch == inner + 2 * SSM_GROUPS * SSM_STATE
    assert 2 * hgdn + hssm <= LANE and (inner // SSM_GROUPS) % (2 * SSM_P) == 0

    lreal = N_META + seq
    lp = -(-(lreal + CONV_K - 1) // ATT_BLOCK) * ATT_BLOCK
    pad = lp - lreal
    assert (lp // 2) % CHUNK == 0

    o_sb = 0
    o_gq = 4 * sbw
    o_gz = o_gq + 2 * qkw + vw
    o_gb = o_gz + vw
    o_ga = o_gb + hgdn
    o_sz = o_ga + hgdn
    o_sx = o_sz + inner
    o_dt = o_sx + conv_ch
    o_gt = o_dt + hssm
    assert w_in.shape[2] == o_gt + 3 * d
    w_sq, w_skvz, w_gz, w_gt, w_sz = (
        w_in[:, :, a:b].astype(BF16)
        for a, b in ((o_sb, o_sb + sbw), (o_sb + sbw, o_gq), (o_gz, o_gb), (o_gt, o_gt + 3 * d), (o_sz, o_sx)))
    w_gconv = w_in[:, :, o_gq:o_gz].astype(BF16)
    w_sconv = w_in[:, :, o_sx:o_dt].astype(BF16)
    nsmall = 2 * hgdn + hssm
    w_small = jnp.concatenate([w_in[:, :, o_gb:o_sz], w_in[:, :, o_dt:o_gt],
                               jnp.zeros((depth, d, LANE - nsmall), F32)], axis=2).astype(BF16)
    zpad = lambda t: jnp.concatenate([t, jnp.zeros((depth, LANE - t.shape[1]), F32)], axis=1)
    kind = jnp.concatenate([jnp.zeros((hgdn,), F32), jnp.ones((hgdn,), F32), jnp.full((hssm,), 2.0, F32),
                            jnp.full((LANE - nsmall,), 3.0, F32)])
    small_par = jnp.stack([
        jnp.broadcast_to(kind, (depth, LANE)),
        zpad(jnp.concatenate([jnp.zeros((depth, hgdn), F32), gdn_a_log], axis=1)),
        zpad(jnp.concatenate([jnp.zeros((depth, hgdn), F32), gdn_dt_bias, ssm_dt_bias], axis=1)),
    ] + [jnp.zeros((depth, LANE), F32)] * 5, axis=1)
    wa, wb, wc, wo = (t.astype(BF16) for t in (w_branch_a, w_branch_b, w_branch_c, w_out))
    gconv_b = jnp.zeros((1, w_gconv.shape[2]), F32)

    meta = jnp.broadcast_to(meta_tokens.astype(x.dtype)[None], (bsz, N_META, d))
    h = jnp.concatenate([jnp.zeros((bsz, pad, d), x.dtype), meta, x], axis=1).reshape(bsz * lp, d)
    xn = _rmsnorm(h, norm_g[0], BF16)

    plain = lambda w, name, dt: _inproj(xn, w, bsz, lp, tn=_pick(w.shape[1], (1024, 512, 256)),
                                        kernel=_inproj_plain_kernel, out_dtype=dt, name=name)
    tn_g = _pick(qkw, (256, 128))
    tn_s = _pick(conv_ch, (256, 128))
    conv_specs = lambda tn: (pl.BlockSpec((CONV_K, tn), lambda b, j: (0, j)),
                             pl.BlockSpec((1, tn), lambda b, j: (0, j)))
    for layer in range(depth):
        sbq = plain(w_sq[layer], "inproj_sb_q", F32)
        sbkvz = plain(w_skvz[layer], "inproj_sb_kvz", BF16)
        gdz = plain(w_gz[layer], "inproj_gdn_z", BF16)
        gates = plain(w_gt[layer], "inproj_gates", BF16)
        ssz = plain(w_sz[layer], "inproj_ssm_z", BF16)
        gconv = _inproj(xn, w_gconv[layer], bsz, lp, tn=tn_g,
                        kernel=functools.partial(_inproj_conv_kernel, pad=pad, norm_blocks=qkw // tn_g,
                                                 q_scale=HEAD_DIM ** -0.5),
                        extra=(gdn_conv_w[layer], gconv_b), extra_specs=conv_specs(tn_g), name="inproj_gdn_conv")
        sconv = _inproj(xn, w_sconv[layer], bsz, lp, tn=tn_s,
                        kernel=functools.partial(_inproj_conv_kernel, pad=pad, norm_blocks=0, q_scale=1.0),
                        extra=(ssm_conv_w[layer], ssm_conv_b[layer].reshape(1, conv_ch)),
                        extra_specs=conv_specs(tn_s), name="inproj_ssm_conv")
        small = _inproj(xn, w_small[layer], bsz, lp, tn=LANE,
                        kernel=functools.partial(_inproj_small_kernel, pad=pad),
                        extra=(small_par[layer],), extra_specs=(pl.BlockSpec((8, LANE), lambda b, j: (0, 0)),),
                        name="inproj_small")
        oa = _attention(sbq, sbkvz, bsz, lp, hsb, 0, 0, sbw, 2 * sbw)
        ob = _gdn(gconv, small, gdz, gdn_norm_g[layer], bsz, lp, pad, hgdn, 0)
        oc = _ssd(sconv, small, ssz, ssm_a_log[layer], ssm_d[layer], ssm_norm_g[layer], bsz, lp, pad, hgdn,
                  inner, 0)
        last = layer == depth - 1
        g_next = final_norm_g if last else norm_g[layer + 1]
        h, xn = _out_stage(oa, ob, oc, gates, h, wa[layer], wb[layer], wc[layer], wo[layer], g_next, 0,
                           F32 if last else BF16)
    return xn.reshape(bsz, lp, d)[:, pad + N_META:]
```

```python
import functools

import jax
import jax.numpy as jnp
from jax import lax
from jax.experimental import pallas as pl
from jax.experimental.pallas import tpu as pltpu

F32 = jnp.float32
BF16 = jnp.bfloat16

N_META = 16
RMS_EPS = 1e-6
L2_EPS = 1e-6
CONV_K = 4
HEAD_DIM = 128
SSM_GROUPS = 2
SSM_STATE = 128
SSM_P = 64
CHUNK = 64
LANE = 128
ATT_BLOCK = 256
VMEM_LIMIT = 52 * 1024 * 1024
LOG2E = 1.4426950408889634
EXP2_CAP = 126.0
GDN_UNROLL = 18
SSD_SPLIT = 1
GDN_HEADS = 2
ATT_HEADS = 4
ATT_LOG_TERMS = 1


def _cparams(*sem):
    return pltpu.CompilerParams(dimension_semantics=sem, vmem_limit_bytes=VMEM_LIMIT)


def _sigmoid(x):
    return 1.0 / (1.0 + jnp.exp(-x))


def _silu(x):
    return x * _sigmoid(x)


def _softplus(x):
    return jnp.maximum(x, 0.0) + jnp.log1p(jnp.exp(-jnp.abs(x)))


def _split3(x):
    hi = x.astype(BF16)
    r = x - hi.astype(F32)
    mid = r.astype(BF16)
    lo = (r - mid.astype(F32)).astype(BF16)
    return hi, mid, lo


def _dot(a, b):
    return jnp.dot(a, b, preferred_element_type=F32)


def _dot_nt(a, b):
    return lax.dot_general(a, b, (((1,), (1,)), ((), ())), preferred_element_type=F32)


def _dot_tn(a, b):
    return lax.dot_general(a, b, (((0,), (0,)), ((), ())), preferred_element_type=F32)


def _dot_hp(a, b):
    ah = a.astype(BF16)
    al = (a - ah.astype(F32)).astype(BF16)
    bh = b.astype(BF16)
    bl = (b - bh.astype(F32)).astype(BF16)
    return _dot(ah, bh) + (_dot(al, bh) + _dot(ah, bl))


def _pick(n, cands):
    for c in cands:
        if n % c == 0:
            return c
    raise ValueError(f"no tile for {n} in {cands}")


def _rmsnorm_kernel(h_ref, g_ref, o_ref):
    x = h_ref[...]
    y = x * lax.rsqrt(jnp.mean(x * x, axis=-1, keepdims=True) + RMS_EPS) * g_ref[...]
    o_ref[...] = y.astype(o_ref.dtype)


def _rmsnorm(h, g, out_dtype):
    tp, d = h.shape
    tm = _pick(tp, (512, 384, 256, 128))
    return pl.pallas_call(
        _rmsnorm_kernel,
        grid=(tp // tm,),
        in_specs=[pl.BlockSpec((tm, d), lambda i: (i, 0)), pl.BlockSpec((1, d), lambda i: (0, 0))],
        out_specs=pl.BlockSpec((tm, d), lambda i: (i, 0)),
        out_shape=jax.ShapeDtypeStruct((tp, d), out_dtype),
        compiler_params=_cparams("parallel"),
        name="rmsnorm",
    )(h, g.reshape(1, d))


def _inproj_plain_kernel(x_ref, w_ref, o_ref):
    o_ref[...] = _dot(x_ref[...], w_ref[...].astype(BF16)).astype(o_ref.dtype)


def _inproj_conv_kernel(x_ref, w_ref, cw_ref, cb_ref, o_ref, *, pad, norm_blocks, q_scale):
    acc = _dot(x_ref[...], w_ref[...].astype(BF16))
    cw = cw_ref[...]
    y = acc * cw[CONV_K - 1:CONV_K, :]
    for k in range(CONV_K - 1):
        y = y + pltpu.roll(acc, CONV_K - 1 - k, axis=0) * cw[k:k + 1, :]
    y = _silu(y + cb_ref[...])
    if norm_blocks:
        j = pl.program_id(1)
        for s in range(y.shape[1] // HEAD_DIM):
            ys = y[:, s * HEAD_DIM:(s + 1) * HEAD_DIM]
            rs = lax.rsqrt(jnp.sum(ys * ys, axis=1, keepdims=True) + L2_EPS)
            fac = jnp.where(j < norm_blocks, rs * q_scale, jnp.where(j < 2 * norm_blocks, rs, 1.0))
            o_ref[:, s * HEAD_DIM:(s + 1) * HEAD_DIM] = (ys * fac).astype(o_ref.dtype)
    else:
        o_ref[...] = y.astype(o_ref.dtype)
    o_ref[0:pad, :] = jnp.zeros((pad, o_ref.shape[1]), o_ref.dtype)


def _inproj_small_kernel(x_ref, w_ref, p_ref, o_ref, *, pad):
    v = _dot(x_ref[...], w_ref[...])
    kind = p_ref[0:1, :]
    alog = p_ref[1:2, :]
    bias = p_ref[2:3, :]
    sp = _softplus(v + bias)
    out = jnp.where(kind == 0.0, _sigmoid(v),
                    jnp.where(kind == 1.0, -jnp.exp(alog) * sp, jnp.where(kind == 2.0, sp, 0.0)))
    row = lax.broadcasted_iota(jnp.int32, out.shape, 0)
    o_ref[...] = jnp.where(row >= pad, out, 0.0)


def _inproj(xn, w, bsz, lp, *, tn, kernel, extra=(), extra_specs=(), out_dtype=F32, name, window=None):
    d = xn.shape[1]
    if window is None:
        n = w.shape[1]
        w_spec = pl.BlockSpec((d, tn), lambda b, j: (0, j))
    else:
        layer, col0, n = window
        assert col0 % tn == 0 and n % tn == 0
        w_spec = pl.BlockSpec((None, d, tn), lambda b, j: (layer, 0, col0 // tn + j))
    return pl.pallas_call(
        kernel,
        grid=(bsz, n // tn),
        in_specs=[pl.BlockSpec((lp, d), lambda b, j: (b, 0)), w_spec, *extra_specs],
        out_specs=pl.BlockSpec((lp, tn), lambda b, j: (b, j)),
        out_shape=jax.ShapeDtypeStruct((bsz * lp, n), out_dtype),
        compiler_params=_cparams("parallel", "arbitrary"),
        name=name,
    )(xn, w, *extra)


def _attn_kernel(q_ref, k_ref, v_ref, z_ref, uu_ref, o_ref, *, blk, nhb, scale):
    i = pl.program_id(2)
    uu = uu_ref[...]
    causal = (lax.broadcasted_iota(jnp.int32, (blk, blk), 1) < lax.broadcasted_iota(jnp.int32, (blk, blk), 0))
    qs = [(q_ref[:, hh * HEAD_DIM:(hh + 1) * HEAD_DIM] * (-scale * LOG2E)).astype(BF16) for hh in range(nhb)]

    def logits(hh, j):
        r0 = pl.multiple_of(j * blk, blk)
        kb = k_ref[pl.ds(r0, blk), hh * HEAD_DIM:(hh + 1) * HEAD_DIM].astype(BF16)
        return _dot_nt(qs[hh], kb)

    def log_terms(zn, diag):
        zn = jnp.minimum(zn, EXP2_CAP)
        sp = jnp.log2(1.0 + jnp.exp2(zn))
        lk = zn - sp
        if diag:
            lk = jnp.where(causal, lk, 0.0)
        terms, resid = [], lk
        for t in range(ATT_LOG_TERMS):
            terms.append(resid.astype(BF16))
            if t + 1 < ATT_LOG_TERMS:
                resid = resid - terms[-1].astype(F32)
        return sp, lk, jnp.concatenate(terms, axis=1) if len(terms) > 1 else terms[0]

    def weights(sp, later, diag):
        w = jnp.exp2(later - sp)
        if diag:
            w = jnp.where(causal, w, 0.0)
        return w.astype(BF16)

    def values(hh, j):
        r0 = pl.multiple_of(j * blk, blk)
        return v_ref[pl.ds(r0, blk), hh * HEAD_DIM:(hh + 1) * HEAD_DIM].astype(BF16)

    def tiles(j, acc, run, diag):
        heads = range(nhb)
        zs = [logits(hh, j) for hh in heads]
        lt = [log_terms(z, diag) for z in zs]
        later = [_dot(lt[hh][2], uu) + run[hh * blk:(hh + 1) * blk] for hh in heads]
        ws = [weights(lt[hh][0], later[hh], diag) for hh in heads]
        pv = [_dot(ws[hh], values(hh, j)) for hh in heads]
        rs = [jnp.sum(lt[hh][1], axis=1, keepdims=True) for hh in heads]
        return acc + jnp.concatenate(pv, axis=0), run + jnp.concatenate(rs, axis=0)

    carry = tiles(i, jnp.zeros((nhb * blk, HEAD_DIM), F32), jnp.zeros((nhb * blk, 1), F32), True)
    acc, _ = lax.fori_loop(1, i + 1, lambda step, c: tiles(i - step, *c, False), carry)
    for hh in range(nhb):
        sl = slice(hh * HEAD_DIM, (hh + 1) * HEAD_DIM)
        o_ref[:, sl] = (acc[hh * blk:(hh + 1) * blk] * _silu(z_ref[:, sl].astype(F32))).astype(o_ref.dtype)


def _attention(qarr, kvz, bsz, lp, nh, off_q, off_k, off_v, off_z):
    blk = ATT_BLOCK
    nq = lp // blk
    nhb = next(n for n in (ATT_HEADS, 2, 1) if nh % n == 0)
    wd = nhb * HEAD_DIM
    j = lax.broadcasted_iota(jnp.int32, (ATT_LOG_TERMS * blk, blk), 0) % blk
    s = lax.broadcasted_iota(jnp.int32, (ATT_LOG_TERMS * blk, blk), 1)
    uu = (j > s).astype(BF16)
    bq, bk, bv, bz = (o // wd for o in (off_q, off_k, off_v, off_z))
    return pl.pallas_call(
        functools.partial(_attn_kernel, blk=blk, nhb=nhb, scale=HEAD_DIM ** -0.5),
        grid=(bsz, nh // nhb, nq),
        in_specs=[pl.BlockSpec((blk, wd), lambda b, h, i: (b * nq + i, bq + h)),
                  pl.BlockSpec((lp, wd), lambda b, h, i: (b, bk + h)),
                  pl.BlockSpec((lp, wd), lambda b, h, i: (b, bv + h)),
                  pl.BlockSpec((blk, wd), lambda b, h, i: (b * nq + i, bz + h)),
                  pl.BlockSpec((ATT_LOG_TERMS * blk, blk), lambda b, h, i: (0, 0))],
        out_specs=pl.BlockSpec((blk, wd), lambda b, h, i: (b * nq + i, h)),
        out_shape=jax.ShapeDtypeStruct((bsz * lp, nh * HEAD_DIM), BF16),
        compiler_params=_cparams("parallel", "parallel", "arbitrary"),
        name="sb_attention",
    )(qarr, kvz, kvz, kvz, uu)


def _gdn_kernel(q_ref, k_ref, v_ref, sm_ref, z_ref, ng_ref, ll3_ref, wx_ref, o_ref,
                a_s, b_s, qp_s, ob_s, eg_s, *, lp, pad, nh, nhb, unroll):
    h0 = pl.program_id(1) * nhb
    cl = CHUNK
    nc = lp // cl
    c0 = pad // cl
    c1 = nc - (nc - c0 + unroll - 1) // unroll * unroll
    assert c1 >= 0
    ll3 = ll3_ref[...]
    wx = (wx_ref[0:cl, :], wx_ref[cl:2 * cl, :])
    li = lax.broadcasted_iota(jnp.int32, (cl, 2 * cl), 0)
    lanew = lax.broadcasted_iota(jnp.int32, (cl, 2 * cl), 1)
    first = lanew < cl
    si = lanew & (cl - 1)
    eye = (li == si).astype(F32)
    lane = lax.broadcasted_iota(jnp.int32, (cl, LANE), 1)
    zero_b = jnp.zeros((cl, 2 * cl), BF16)
    zero_h = jnp.zeros((cl, HEAD_DIM), BF16)
    if c1:
        ob_s[:, 0:c1 * cl, :] = jnp.zeros((nhb, c1 * cl, HEAD_DIM), F32)

    def split2(x):
        hi = x.astype(BF16)
        return hi, (x - hi.astype(F32)).astype(BF16)

    def blockdiag(xb):
        return jnp.concatenate([jnp.where(first, xb, zero_b), jnp.where(first, zero_b, xb)], axis=0)

    def pair_mul(x, bd):
        xh, xl = split2(x)
        return _dot(jnp.concatenate([xh, xl, xh], axis=1), jnp.concatenate([bd[0], bd[0], bd[1]], axis=0))

    def rhs_pair(a0, a1, b0, b1):
        return jnp.concatenate([jnp.concatenate([a0, a1, zero_h, zero_h], axis=1),
                                jnp.concatenate([zero_h, zero_h, b0, b1], axis=1)], axis=0)

    def pass1(hh, i):
        c = c1 + i * unroll
        cs = range(unroll)
        ps = range(unroll // 2)
        hs = slice(hh * HEAD_DIM, (hh + 1) * HEAD_DIM)
        rows = [pl.ds(pl.multiple_of((c + r) * cl, cl), cl) for r in cs]
        q = [q_ref[rw, hs] for rw in rows]
        k = [k_ref[rw, hs] for rw in rows]
        v = [v_ref[rw, hs] for rw in rows]
        sm = [sm_ref[rw, :] for rw in rows]
        beta = [jnp.sum(jnp.where(lane == h0 + hh, x, 0.0), axis=1, keepdims=True) for x in sm]
        g = [jnp.sum(jnp.where(lane == nh + h0 + hh, x, 0.0), axis=1, keepdims=True) for x in sm]
        sg = [_dot(ll3, jnp.concatenate(_split3(g[r] * wx[r % 2]), axis=0)) for r in cs]
        eseg = [jnp.exp(sg[2 * p][0:cl, 0:2 * cl] + sg[2 * p + 1][0:cl, 0:2 * cl]) for p in ps]
        gcb = [x[0:cl, 2 * cl:4 * cl] for x in sg]
        glb = [x[cl:2 * cl, 2 * cl:4 * cl] for x in sg]
        egc = [jnp.exp(x) for x in gcb]
        kb = [k[r] * beta[r] for r in cs]
        kq = [_dot_nt(jnp.concatenate([kb[2 * p], kb[2 * p + 1], q[2 * p], q[2 * p + 1]], axis=0),
                      jnp.concatenate([k[2 * p], k[2 * p + 1]], axis=0)) for p in ps]
        kk = [jnp.where(first, x[0:cl], x[cl:2 * cl]) for x in kq]
        aqk = [(jnp.where(first, kq[p][2 * cl:3 * cl], kq[p][3 * cl:4 * cl])
                * jnp.where(li >= si, eseg[p], 0.0)).astype(BF16) for p in ps]
        pw = [-(kk[p] * jnp.where(li > si, eseg[p], 0.0)) for p in ps]
        tw = [eye + x for x in pw]
        bd = [tuple(blockdiag(y) for y in split2(x)) for x in pw]
        pw = [pair_mul(pw[p], bd[p]) for p in ps]
        for rnd in range(5):
            bd = [tuple(blockdiag(y) for y in split2(x)) for x in pw]
            if rnd < 4:
                both = [pair_mul(jnp.concatenate([pw[p], tw[p]], axis=0), bd[p]) for p in ps]
                pw = [x[0:cl] for x in both]
                tw = [tw[p] + both[p][cl:2 * cl] for p in ps]
            else:
                tw = [tw[p] + pair_mul(tw[p], bd[p]) for p in ps]
        vb = [(v[r] * beta[r]).astype(BF16) for r in cs]
        kbe = [(kb[r] * egc[r]).astype(BF16) for r in cs]
        uw = [_dot(tw[p].astype(BF16), rhs_pair(kbe[2 * p], vb[2 * p], kbe[2 * p + 1], vb[2 * p + 1])) for p in ps]
        wu = [uw[r // 2][:, (r % 2) * 2 * HEAD_DIM:(r % 2 + 1) * 2 * HEAD_DIM] for r in cs]
        wub = [x.astype(BF16) for x in wu]
        awu = [_dot(aqk[p], rhs_pair(wub[2 * p][:, 0:HEAD_DIM], wub[2 * p][:, HEAD_DIM:],
                                     wub[2 * p + 1][:, 0:HEAD_DIM], wub[2 * p + 1][:, HEAD_DIM:])) for p in ps]
        ke = [k[r] * jnp.exp(glb[r] - gcb[r]) for r in cs]
        ab = [_dot_tn(ke[r].astype(BF16), wub[r]) for r in cs]
        aw = [awu[r // 2][:, (r % 2) * 2 * HEAD_DIM:(r % 2) * 2 * HEAD_DIM + HEAD_DIM] for r in cs]
        au = [awu[r // 2][:, (r % 2) * 2 * HEAD_DIM + HEAD_DIM:(r % 2 + 1) * 2 * HEAD_DIM] for r in cs]
        outs = ([x[:, 0:HEAD_DIM].astype(a_s.dtype) for x in ab], [x[:, HEAD_DIM:] for x in ab],
                [(q[r] * egc[r] - aw[r]).astype(qp_s.dtype) for r in cs], au,
                [jnp.exp(x[0:8, :]) for x in glb])
        for ref, nr, vals in zip((a_s, b_s, qp_s, ob_s, eg_s), (HEAD_DIM, HEAD_DIM, cl, cl, 8), outs):
            ref[hh, pl.ds(pl.multiple_of(c * nr, nr), unroll * nr), :] = jnp.concatenate(vals, axis=0)

    for hh in range(nhb):
        lax.fori_loop(0, (nc - c1) // unroll, lambda i, carry, hh=hh: (pass1(hh, i), carry)[1], 0)

    def pass2(c, states):
        heads = range(nhb)
        r0 = pl.ds(pl.multiple_of(c * cl, cl), cl)
        rk = pl.ds(pl.multiple_of(c * HEAD_DIM, HEAD_DIM), HEAD_DIM)
        r8 = pl.ds(pl.multiple_of(c * 8, 8), 8)
        lhs = [jnp.concatenate([a_s[hh, rk, :], qp_s[hh, r0, :]], axis=0) for hh in heads]
        mm = [_dot(lhs[hh], states[hh].astype(BF16)) for hh in heads]
        for hh in heads:
            ob_s[hh, r0, :] = ob_s[hh, r0, :] + mm[hh][HEAD_DIM:, :]
        return tuple(states[hh] * eg_s[hh, r8, :][0:1, :] - mm[hh][0:HEAD_DIM, :] + b_s[hh, rk, :] for hh in heads)

    lax.fori_loop(c0, nc, pass2, tuple(jnp.zeros((HEAD_DIM, HEAD_DIM), F32) for _ in range(nhb)))

    for hh in range(nhb):
        hs = slice(hh * HEAD_DIM, (hh + 1) * HEAD_DIM)
        o = ob_s[hh]
        on = o * lax.rsqrt(jnp.mean(o * o, axis=1, keepdims=True) + RMS_EPS) * ng_ref[...]
        o_ref[:, hs] = (on * _silu(z_ref[:, hs].astype(F32))).astype(o_ref.dtype)


def _gdn(gconv, small, plain, ng, bsz, lp, pad, nh, off_z):
    cl = CHUNK
    nc = lp // cl
    n = nc - pad // cl
    unroll = next(u for u in range(min(GDN_UNROLL, nc), 0, -2) if -(-n // u) * u <= nc)
    assert unroll % 2 == 0
    l = lax.broadcasted_iota(jnp.int32, (2 * cl, 3 * cl), 0)
    jj = lax.broadcasted_iota(jnp.int32, (2 * cl, 3 * cl), 1) % cl
    ll3 = ((l >= cl) | (jj <= l)).astype(BF16)
    j2 = lax.broadcasted_iota(jnp.int32, (2 * cl, 4 * cl), 0)
    s2 = lax.broadcasted_iota(jnp.int32, (2 * cl, 4 * cl), 1)
    tri = (s2 // cl == j2 // cl) & (j2 % cl > s2 % cl)
    wx = jnp.where(s2 < 2 * cl, tri.astype(F32), 1.0)
    nhb = next(m for m in (GDN_HEADS, 1) if nh % m == 0)
    wd = nhb * HEAD_DIM
    bz = off_z // wd
    nb = nh // nhb
    seq = lambda off: pl.BlockSpec((lp, wd), lambda b, h: (b, off + h))
    const = lambda shp: pl.BlockSpec(shp, lambda b, h: (0, 0))
    return pl.pallas_call(
        functools.partial(_gdn_kernel, lp=lp, pad=pad, nh=nh, nhb=nhb, unroll=unroll),
        grid=(bsz, nb),
        in_specs=[seq(0), seq(nb), seq(2 * nb),
                  pl.BlockSpec((lp, LANE), lambda b, h: (b, 0)),
                  seq(bz), const((1, HEAD_DIM)), const((2 * cl, 3 * cl)), const((2 * cl, 4 * cl))],
        out_specs=pl.BlockSpec((lp, wd), lambda b, h: (b, h)),
        out_shape=jax.ShapeDtypeStruct((bsz * lp, nh * HEAD_DIM), BF16),
        scratch_shapes=[pltpu.VMEM((nhb, nc * HEAD_DIM, HEAD_DIM), BF16),
                        pltpu.VMEM((nhb, nc * HEAD_DIM, HEAD_DIM), F32),
                        pltpu.VMEM((nhb, lp, HEAD_DIM), BF16), pltpu.VMEM((nhb, lp, HEAD_DIM), F32),
                        pltpu.VMEM((nhb, nc * 8, HEAD_DIM), F32)],
        compiler_params=_cparams("parallel", "parallel"),
        name="gated_deltanet",
    )(gconv, gconv, gconv, small, plain, ng.reshape(1, HEAD_DIM), ll3, wx)


def _ssd_kernel(x_ref, b_ref, c_ref, sm_ref, z_ref, esel_ref, alog_ref, d_ref, ng_ref, ll_ref, wt_ref,
                incl_ref, o_ref, s_ref, *, rows, hg, skip):
    cl = CHUNK
    wd = hg * SSM_P
    groups = range(SSM_GROUPS)

    @pl.when(pl.program_id(1) == 0)
    def _():
        s_ref[...] = jnp.zeros_like(s_ref)

    wu = wd // SSD_SPLIT
    ll = ll_ref[...]
    wt = wt_ref[:, 0:wu]
    incl = incl_ref[:, 0:wu]
    lane = lax.broadcasted_iota(jnp.int32, (cl, LANE), 1)
    zero_b = jnp.zeros((cl, LANE), BF16)

    def split2(x):
        hi = x.astype(BF16)
        return hi, (x - hi.astype(F32)).astype(BF16)

    def walk(c, units):
        rw = pl.ds(pl.multiple_of(c * cl, cl), cl)
        us = range(len(units))
        gof = [g for g, _ in units]
        ls = [slice(hf * wu, (hf + 1) * wu) for _, hf in units]
        gl = [slice(g * wd + hf * wu, g * wd + (hf + 1) * wu) for g, hf in units]
        x = [x_ref[rw, gl[u]] for u in us]
        bm = [b_ref[rw, g * SSM_STATE:(g + 1) * SSM_STATE] for g in groups]
        cm = [c_ref[rw, g * SSM_STATE:(g + 1) * SSM_STATE] for g in groups]
        sm2 = jnp.concatenate(split2(sm_ref[rw, :]), axis=1)
        dt = [_dot(sm2, esel_ref[gof[u], :, ls[u]]) for u in us]
        la = [dt[u] * -jnp.exp(alog_ref[gof[u], :, ls[u]]) for u in us]
        xs = [x[u] * dt[u] for u in us]
        lab = [split2(la[u]) for u in us]
        sg = [_dot(ll, jnp.concatenate([jnp.concatenate([t * wt, t], axis=1) for t in lab[u]], axis=0))
              for u in us]
        decay = [jnp.where(incl > 0.0, jnp.exp(sg[u][0:cl, 0:wu]), 0.0) for u in us]
        cs = [sg[u][0:cl, wu:2 * wu] for u in us]
        tot = [sg[u][cl:2 * cl, wu:2 * wu] for u in us]
        btile = [jnp.concatenate([bm[g]] * (wu // cl), axis=0) for g in groups]
        scores = [_dot_nt(cm[gof[u]], btile[gof[u]]) for u in us]
        mm = [(scores[u] * decay[u]).astype(BF16) for u in us]
        xb = [xs[u].astype(BF16) for u in us]
        y_diag = []
        for u in us:
            ys = []
            for pp in range(wu // LANE):
                xp = xb[u][:, pp * LANE:(pp + 1) * LANE]
                bd = jnp.concatenate([jnp.where(lane < SSM_P, xp, zero_b), jnp.where(lane < SSM_P, zero_b, xp)],
                                     axis=0)
                ys.append(_dot(mm[u][:, pp * LANE:(pp + 1) * LANE], bd))
            y_diag.append(jnp.concatenate(ys, axis=1))
        state = [s_ref[gof[u], :, ls[u]] for u in us]
        y_off = [_dot(cm[gof[u]], state[u]) * jnp.exp(cs[u]) for u in us]
        ds = [_dot_tn(bm[gof[u]], xs[u] * jnp.exp(tot[u] - cs[u])) for u in us]
        for u in us:
            s_ref[gof[u], :, ls[u]] = state[u] * jnp.exp(tot[u][0:1, :]) + ds[u]
        y = [(y_diag[u] + y_off[u] + d_ref[gof[u], :, ls[u]] * x[u]) * _silu(z_ref[rw, gl[u]].astype(F32))
             for u in us]
        sq = [jnp.sum(y[u] * y[u], axis=1, keepdims=True) for u in us]
        for g in sorted(set(gof)):
            rs = lax.rsqrt(sum(sq[u] for u in us if gof[u] == g) * (1.0 / wd) + RMS_EPS)
            for u in us:
                if gof[u] == g:
                    o_ref[rw, gl[u]] = (y[u] * rs * ng_ref[g, :, ls[u]]).astype(o_ref.dtype)

    def body(c, carry):
        walk(c, [(g, hf) for g in groups for hf in range(SSD_SPLIT)])
        return carry

    first = pl.program_id(1) == 0
    if skip:
        @pl.when(first)
        def _():
            o_ref[0:skip * cl, :] = jnp.zeros((skip * cl, o_ref.shape[1]), o_ref.dtype)

    lax.fori_loop(jnp.where(first, skip, 0), rows // cl, body, 0)


def _ssd(sconv, small, plain, alog, dvec, ng, bsz, lp, pad, n_gdn_heads, inner, off_z):
    cl = CHUNK
    ng_ = SSM_GROUPS
    wd = inner // ng_
    hg = wd // SSM_P
    ns = 4
    rows = lp // ns
    assert rows % cl == 0 and off_z % inner == 0 and inner % (ng_ * SSM_STATE) == 0
    col = lax.broadcasted_iota(jnp.int32, (ng_, 2 * LANE, wd), 1) % LANE
    hh = lax.broadcasted_iota(jnp.int32, (ng_, 2 * LANE, wd), 2) // SSM_P
    gi = lax.broadcasted_iota(jnp.int32, (ng_, 2 * LANE, wd), 0)
    esel = (col == 2 * n_gdn_heads + gi * hg + hh).astype(BF16)
    l = lax.broadcasted_iota(jnp.int32, (2 * cl, 2 * cl), 0)
    jj = lax.broadcasted_iota(jnp.int32, (2 * cl, 2 * cl), 1) % cl
    ll = ((l >= cl) | (jj <= l)).astype(BF16)
    j2 = lax.broadcasted_iota(jnp.int32, (cl, wd), 0)
    s2 = lax.broadcasted_iota(jnp.int32, (cl, wd), 1) % cl
    wt = (j2 > s2).astype(BF16)
    incl = (s2 <= j2).astype(F32)
    expand = lambda t: jnp.repeat(t.reshape(ng_, 1, hg), SSM_P, axis=2)
    nbc = ng_ * SSM_STATE
    rowblk = lambda w, cb: pl.BlockSpec((rows, w), lambda b, s: (b * ns + s, cb))
    full = lambda shp: pl.BlockSpec(shp, lambda b, s: (0,) * len(shp))
    return pl.pallas_call(
        functools.partial(_ssd_kernel, rows=rows, hg=hg, skip=min(pad // cl, rows // cl)),
        grid=(bsz, ns),
        in_specs=[rowblk(inner, 0), rowblk(nbc, inner // nbc), rowblk(nbc, inner // nbc + 1), rowblk(LANE, 0),
                  rowblk(inner, off_z // inner),
                  full((ng_, 2 * LANE, wd)), full((ng_, 1, wd)), full((ng_, 1, wd)), full((ng_, 1, wd)),
                  full((2 * cl, 2 * cl)), full((cl, wd)), full((cl, wd))],
        out_specs=rowblk(inner, 0),
        out_shape=jax.ShapeDtypeStruct((bsz * lp, inner), BF16),
        scratch_shapes=[pltpu.VMEM((ng_, SSM_STATE, wd), F32)],
        compiler_params=_cparams("parallel", "arbitrary"),
        name="ssd",
    )(sconv, sconv, sconv, small, plain, esel, expand(alog), expand(dvec), ng.reshape(ng_, 1, wd),
      ll, wt, incl)


def _out_kernel(oa_ref, ob_ref, oc_ref, ga_ref, gb_ref, gc_ref, h_ref, wa_ref, wb_ref, wc_ref, wo_ref,
                g_ref, hn_ref, xn_ref):
    merged = (_sigmoid(ga_ref[...].astype(F32)) * _dot(oa_ref[...], wa_ref[...])
              + _sigmoid(gb_ref[...].astype(F32)) * _dot(ob_ref[...], wb_ref[...])
              + _sigmoid(gc_ref[...].astype(F32)) * _dot(oc_ref[...], wc_ref[...]))
    hn = h_ref[...] + _dot(merged.astype(BF16), wo_ref[...])
    hn_ref[...] = hn
    y = hn * lax.rsqrt(jnp.mean(hn * hn, axis=-1, keepdims=True) + RMS_EPS) * g_ref[...]
    xn_ref[...] = y.astype(xn_ref.dtype)


def _out_stage(oa, ob, oc, plain, h, wa, wb, wc, wo, g_next, off_gates, xn_dtype):
    tp, d = h.shape
    tm = _pick(tp, (256, 128))
    bg = off_gates // d
    row = lambda w, cb=0: pl.BlockSpec((tm, w), lambda i: (i, cb))
    full = lambda a: pl.BlockSpec(a.shape, lambda i: (0, 0))
    return pl.pallas_call(
        _out_kernel,
        grid=(tp // tm,),
        in_specs=[row(oa.shape[1]), row(ob.shape[1]), row(oc.shape[1]),
                  row(d, bg), row(d, bg + 1), row(d, bg + 2), row(d),
                  full(wa), full(wb), full(wc), full(wo), pl.BlockSpec((1, d), lambda i: (0, 0))],
        out_specs=[row(d), row(d)],
        out_shape=[jax.ShapeDtypeStruct((tp, d), F32), jax.ShapeDtypeStruct((tp, d), xn_dtype)],
        compiler_params=_cparams("parallel"),
        name="merge_out",
    )(oa, ob, oc, plain, plain, plain, h, wa, wb, wc, wo, g_next.reshape(1, d))


def kernel(x, meta_tokens, norm_g, w_in, gdn_conv_w, gdn_a_log, gdn_dt_bias, gdn_norm_g, ssm_conv_w,
           ssm_conv_b, ssm_a_log, ssm_dt_bias, ssm_d, ssm_norm_g, w_branch_a, w_branch_b, w_branch_c,
           w_out, final_norm_g):
    bsz, seq, d = x.shape
    depth = norm_g.shape[0]
    sbw = w_branch_a.shape[1]
    vw = w_branch_b.shape[1]
    inner = w_branch_c.shape[1]
    hgdn = gdn_a_log.shape[1]
    hssm = ssm_a_log.shape[1]
    conv_ch = ssm_conv_w.shape[2]
    qkw = (gdn_conv_w.shape[2] - vw) // 2
    hsb = sbw // HEAD_DIM
    assert sbw == d and vw == d and qkw == hgdn * HEAD_DIM and vw == hgdn * HEAD_DIM
    assert inner == hssm * SSM_P and conv_ch == inner + 2 * SSM_GROUPS * SSM_STATE
    assert 2 * hgdn + hssm <= LANE and (inner // SSM_GROUPS) % (2 * SSM_P) == 0

    lreal = N_META + seq
    lp = -(-(lreal + CONV_K - 1) // ATT_BLOCK) * ATT_BLOCK
    pad = lp - lreal
    assert (lp // 2) % CHUNK == 0

    o_sb = 0
    o_gq = 4 * sbw
    o_gz = o_gq + 2 * qkw + vw
    o_gb = o_gz + vw
    o_ga = o_gb + hgdn
    o_sz = o_ga + hgdn
    o_sx = o_sz + inner
    o_dt = o_sx + conv_ch
    o_gt = o_dt + hssm
    assert w_in.shape[2] == o_gt + 3 * d
    w_gt, w_sz = (w_in[:, :, a:b].astype(BF16) for a, b in ((o_gt, o_gt + 3 * d), (o_sz, o_sx)))
    w_sconv = w_in[:, :, o_sx:o_dt].astype(BF16)
    nsmall = 2 * hgdn + hssm
    w_small = jnp.concatenate([w_in[:, :, o_gb:o_sz], w_in[:, :, o_dt:o_gt],
                               jnp.zeros((depth, d, LANE - nsmall), F32)], axis=2).astype(BF16)
    zpad = lambda t: jnp.concatenate([t, jnp.zeros((depth, LANE - t.shape[1]), F32)], axis=1)
    kind = jnp.concatenate([jnp.zeros((hgdn,), F32), jnp.ones((hgdn,), F32), jnp.full((hssm,), 2.0, F32),
                            jnp.full((LANE - nsmall,), 3.0, F32)])
    small_par = jnp.stack([
        jnp.broadcast_to(kind, (depth, LANE)),
        zpad(jnp.concatenate([jnp.zeros((depth, hgdn), F32), gdn_a_log], axis=1)),
        zpad(jnp.concatenate([jnp.zeros((depth, hgdn), F32), gdn_dt_bias, ssm_dt_bias], axis=1)),
    ] + [jnp.zeros((depth, LANE), F32)] * 5, axis=1)
    wa, wb, wc, wo = (t.astype(BF16) for t in (w_branch_a, w_branch_b, w_branch_c, w_out))
    gconv_b = jnp.zeros((1, o_gz - o_gq), F32)

    meta = jnp.broadcast_to(meta_tokens.astype(x.dtype)[None], (bsz, N_META, d))
    h = jnp.concatenate([jnp.zeros((bsz, pad, d), x.dtype), meta, x], axis=1).reshape(bsz * lp, d)
    xn = _rmsnorm(h, norm_g[0], BF16)

    plain = lambda w, name, dt: _inproj(xn, w, bsz, lp, tn=_pick(w.shape[1], (1024, 512, 256)),
                                        kernel=_inproj_plain_kernel, out_dtype=dt, name=name)
    inplace = lambda layer, a, b, name, dt: _inproj(
        xn, w_in, bsz, lp, tn=_pick(b - a, (1024, 512, 256)), kernel=_inproj_plain_kernel, out_dtype=dt, name=name,
        window=(layer, a, b - a))
    tn_g = _pick(qkw, (256, 128))
    tn_s = _pick(conv_ch, (256, 128))
    conv_specs = lambda tn: (pl.BlockSpec((CONV_K, tn), lambda b, j: (0, j)),
                             pl.BlockSpec((1, tn), lambda b, j: (0, j)))
    for layer in range(depth):
        sbq = inplace(layer, o_sb, o_sb + sbw, "inproj_sb_q", F32)
        sbkvz = inplace(layer, o_sb + sbw, o_gq, "inproj_sb_kvz", BF16)
        gdz = inplace(layer, o_gz, o_gb, "inproj_gdn_z", BF16)
        gates = plain(w_gt[layer], "inproj_gates", BF16)
        ssz = plain(w_sz[layer], "inproj_ssm_z", BF16)
        gconv = _inproj(xn, w_in, bsz, lp, tn=tn_g,
                        kernel=functools.partial(_inproj_conv_kernel, pad=pad, norm_blocks=qkw // tn_g,
                                                 q_scale=HEAD_DIM ** -0.5),
                        extra=(gdn_conv_w[layer], gconv_b), extra_specs=conv_specs(tn_g), name="inproj_gdn_conv",
                        window=(layer, o_gq, o_gz - o_gq))
        sconv = _inproj(xn, w_sconv[layer], bsz, lp, tn=tn_s,
                        kernel=functools.partial(_inproj_conv_kernel, pad=pad, norm_blocks=0, q_scale=1.0),
                        extra=(ssm_conv_w[layer], ssm_conv_b[layer].reshape(1, conv_ch)),
                        extra_specs=conv_specs(tn_s), name="inproj_ssm_conv")
        small = _inproj(xn, w_small[layer], bsz, lp, tn=LANE,
                        kernel=functools.partial(_inproj_small_kernel, pad=pad),
                        extra=(small_par[layer],), extra_specs=(pl.BlockSpec((8, LANE), lambda b, j: (0, 0)),),
                        name="inproj_small")
        oa = _attention(sbq, sbkvz, bsz, lp, hsb, 0, 0, sbw, 2 * sbw)
        ob = _gdn(gconv, small, gdz, gdn_norm_g[layer], bsz, lp, pad, hgdn, 0)
        oc = _ssd(sconv, small, ssz, ssm_a_log[layer], ssm_d[layer], ssm_norm_g[layer], bsz, lp, pad, hgdn,
                  inner, 0)
        last = layer == depth - 1
        g_next = final_norm_g if last else norm_g[layer + 1]
        h, xn = _out_stage(oa, ob, oc, gates, h, wa[layer], wb[layer], wc[layer], wo[layer], g_next, 0,
                           F32 if last else BF16)
    return xn.reshape(bsz, lp, d)[:, pad + N_META:]
```
